```python
import math
import jax, jax.numpy as jnp
from jax import lax
import numpy as np

D_MODEL = 1024
BATCH = 8
SEQ = 4096
DEPTH = 2

N_META = 16
Q_BLOCK = 128
ROPE_THETA = 500000.0
RMS_EPS = 1e-6

MLA_HEADS = 8
MLA_NOPE = 64
MLA_ROPE = 32
MLA_V = 64
MLA_Q_RANK = 384
MLA_KV_RANK = 256

DIFF_HEADS = 4
DIFF_DH = 64
DIFF_ROPE = DIFF_DH // 4

D_FF = -(-(8 * D_MODEL) // (3 * 256)) * 256

COL_Q_LAT = MLA_Q_RANK
COL_KV_LAT = MLA_KV_RANK
COL_K_ROPE = MLA_ROPE
COL_DIFF_Q = DIFF_HEADS * 2 * DIFF_DH
COL_DIFF_K = DIFF_HEADS * 2 * DIFF_DH
COL_DIFF_V = DIFF_HEADS * 2 * DIFF_DH
COL_GATE_A = D_MODEL
COL_GATE_B = D_MODEL
IN_COLS = COL_Q_LAT + COL_KV_LAT + COL_K_ROPE + COL_DIFF_Q + COL_DIFF_K + COL_DIFF_V + COL_GATE_A + COL_GATE_B
SPLIT_POINTS = (
    COL_Q_LAT,
    COL_Q_LAT + COL_KV_LAT,
    COL_Q_LAT + COL_KV_LAT + COL_K_ROPE,
    COL_Q_LAT + COL_KV_LAT + COL_K_ROPE + COL_DIFF_Q,
    COL_Q_LAT + COL_KV_LAT + COL_K_ROPE + COL_DIFF_Q + COL_DIFF_K,
    COL_Q_LAT + COL_KV_LAT + COL_K_ROPE + COL_DIFF_Q + COL_DIFF_K + COL_DIFF_V,
    COL_Q_LAT + COL_KV_LAT + COL_K_ROPE + COL_DIFF_Q + COL_DIFF_K + COL_DIFF_V + COL_GATE_A,
)

kernel_name = "hybrid_mla_diffattn_gated_block"


def rms_norm(x, gain):
    xf = x.astype(jnp.float32)
    y = xf * lax.rsqrt(jnp.mean(xf * xf, axis=-1, keepdims=True) + RMS_EPS)
    return y.astype(x.dtype) * gain.astype(x.dtype)


def rope_tables(length, rot_dim, dtype):
    inv = ROPE_THETA ** (-jnp.arange(0, rot_dim, 2, dtype=jnp.float32) / rot_dim)
    ang = jnp.arange(length, dtype=jnp.float32)[:, None] * inv[None, :]
    return jnp.cos(ang).astype(dtype), jnp.sin(ang).astype(dtype)


def apply_rope(t, cos, sin):
    half = cos.shape[-1]
    c = cos[None, :, None, :]
    s = sin[None, :, None, :]
    t1, t2, rest = t[..., :half], t[..., half:2 * half], t[..., 2 * half:]
    return jnp.concatenate([t1 * c - t2 * s, t2 * c + t1 * s, rest], axis=-1)


def to_query_blocks(t):
    b, l, h, d = t.shape
    return t.reshape(b, l // Q_BLOCK, Q_BLOCK, h, d).transpose(1, 0, 3, 2, 4)


def from_query_blocks(t):
    nb, b, h, qb, d = t.shape
    return t.transpose(1, 0, 3, 2, 4).reshape(b, nb * qb, h * d)


def causal_probs(qb, k, start, scale):
    s = jnp.einsum('bhqd,bhkd->bhqk', qb, k).astype(jnp.float32) * scale
    qpos = start + jnp.arange(qb.shape[2])
    kpos = jnp.arange(k.shape[2])
    mask = kpos[None, :] <= qpos[:, None]
    s = jnp.where(mask[None, None], s, -jnp.inf)
    return jax.nn.softmax(s, axis=-1)


def causal_attention(q, k, v, scale):
    kt = k.transpose(0, 2, 1, 3)
    vt = v.transpose(0, 2, 1, 3)
    qblocks = to_query_blocks(q)
    starts = jnp.arange(qblocks.shape[0]) * Q_BLOCK

    def one_block(args):
        qb, start = args
        p = causal_probs(qb, kt, start, scale)
        return jnp.einsum('bhqk,bhkd->bhqd', p.astype(vt.dtype), vt)

    return from_query_blocks(lax.map(one_block, (qblocks, starts)))


def causal_diff_attention(q1, k1, q2, k2, v, lam, scale):
    k1t = k1.transpose(0, 2, 1, 3)
    k2t = k2.transpose(0, 2, 1, 3)
    vt = v.transpose(0, 2, 1, 3)
    q1b = to_query_blocks(q1)
    q2b = to_query_blocks(q2)
    starts = jnp.arange(q1b.shape[0]) * Q_BLOCK

    def one_block(args):
        qa, qc, start = args
        p = causal_probs(qa, k1t, start, scale) - lam * causal_probs(qc, k2t, start, scale)
        return jnp.einsum('bhqk,bhkd->bhqd', p.astype(vt.dtype), vt)

    out = lax.map(one_block, (q1b, q2b, starts))
    nb, b, h, qb, d = out.shape
    return out.transpose(1, 0, 3, 2, 4).reshape(b, nb * qb, h, d)


def setup_inputs(seed: int = 0) -> dict:
    key = jax.random.key(seed)
    ks = jax.random.split(key, 20)
    f32 = jnp.float32

    def w(k, shape, fan_in):
        return jax.random.normal(k, shape, f32) * (fan_in ** -0.5)

    def gain(k, shape):
        return 1.0 + 0.02 * jax.random.normal(k, shape, f32)

    return {
        "x": jax.random.normal(ks[0], (BATCH, SEQ, D_MODEL), f32),
        "meta_tokens": jax.random.normal(ks[1], (N_META, D_MODEL), f32),
        "norm_mix_pre": gain(ks[2], (DEPTH, D_MODEL)),
        "norm_mix_post": gain(ks[3], (DEPTH, D_MODEL)),
        "norm_ffn_pre": gain(ks[4], (DEPTH, D_MODEL)),
        "norm_ffn_post": gain(ks[5], (DEPTH, D_MODEL)),
        "w_in": w(ks[6], (DEPTH, D_MODEL, IN_COLS), D_MODEL),
        "mla_q_norm": gain(ks[7], (DEPTH, MLA_Q_RANK)),
        "mla_kv_norm": gain(ks[8], (DEPTH, MLA_KV_RANK)),
        "mla_w_uq": w(ks[9], (DEPTH, MLA_Q_RANK, MLA_HEADS * (MLA_NOPE + MLA_ROPE)), MLA_Q_RANK),
        "mla_w_ukv": w(ks[10], (DEPTH, MLA_KV_RANK, MLA_HEADS * (MLA_NOPE + MLA_V)), MLA_KV_RANK),
        "w_o_mla": w(ks[11], (DEPTH, MLA_HEADS * MLA_V, D_MODEL), MLA_HEADS * MLA_V),
        "diff_lambda": 0.1 * jax.random.normal(ks[12], (DEPTH, 4, DIFF_DH), f32),
        "diff_subln": gain(ks[13], (DEPTH, 2 * DIFF_DH)),
        "w_o_diff": w(ks[14], (DEPTH, DIFF_HEADS * 2 * DIFF_DH, D_MODEL), DIFF_HEADS * 2 * DIFF_DH),
        "w_out": w(ks[15], (DEPTH, D_MODEL, D_MODEL), D_MODEL),
        "w_gate_up": w(ks[16], (DEPTH, D_MODEL, 2 * D_FF), D_MODEL),
        "w_down": w(ks[17], (DEPTH, D_FF, D_MODEL), D_FF),
    }


def reference(x, meta_tokens, norm_mix_pre, norm_mix_post, norm_ffn_pre, norm_ffn_post,
              w_in, mla_q_norm, mla_kv_norm, mla_w_uq, mla_w_ukv, w_o_mla,
              diff_lambda, diff_subln, w_o_diff, w_out, w_gate_up, w_down):
    dt = x.dtype
    b, s, d = x.shape
    length = s + N_META
    l_pad = -(-length // Q_BLOCK) * Q_BLOCK
    meta = jnp.broadcast_to(meta_tokens.astype(dt)[None], (b, N_META, d))
    r = jnp.concatenate([meta, x, jnp.zeros((b, l_pad - length, d), dt)], axis=1)

    cos_mla, sin_mla = rope_tables(l_pad, MLA_ROPE, dt)
    cos_dif, sin_dif = rope_tables(l_pad, DIFF_ROPE, dt)
    mla_scale = 1.0 / math.sqrt(MLA_NOPE + MLA_ROPE)
    diff_scale = 1.0 / math.sqrt(DIFF_DH)

    for layer in range(DEPTH):
        h = rms_norm(r, norm_mix_pre[layer])
        proj = h @ w_in[layer].astype(dt)
        (q_lat, kv_lat, k_pe, dq, dk, dv, g_a, g_b) = jnp.split(proj, SPLIT_POINTS, axis=-1)

        c_q = rms_norm(q_lat, mla_q_norm[layer])
        qa = (c_q @ mla_w_uq[layer].astype(dt)).reshape(b, l_pad, MLA_HEADS, MLA_NOPE + MLA_ROPE)
        qa = jnp.concatenate([qa[..., :MLA_NOPE], apply_rope(qa[..., MLA_NOPE:], cos_mla, sin_mla)], axis=-1)
        c_kv = rms_norm(kv_lat, mla_kv_norm[layer])
        kva = (c_kv @ mla_w_ukv[layer].astype(dt)).reshape(b, l_pad, MLA_HEADS, MLA_NOPE + MLA_V)
        k_nope, va = kva[..., :MLA_NOPE], kva[..., MLA_NOPE:]
        k_rope = apply_rope(k_pe[:, :, None, :], cos_mla, sin_mla)
        ka = jnp.concatenate(
            [k_nope, jnp.broadcast_to(k_rope, (b, l_pad, MLA_HEADS, MLA_ROPE))], axis=-1)
        out_a = causal_attention(qa, ka, va, mla_scale) @ w_o_mla[layer].astype(dt)

        qd = apply_rope(dq.reshape(b, l_pad, DIFF_HEADS * 2, DIFF_DH), cos_dif, sin_dif)
        kd = apply_rope(dk.reshape(b, l_pad, DIFF_HEADS * 2, DIFF_DH), cos_dif, sin_dif)
        qd = qd.reshape(b, l_pad, DIFF_HEADS, 2, DIFF_DH)
        kd = kd.reshape(b, l_pad, DIFF_HEADS, 2, DIFF_DH)
        vd = dv.reshape(b, l_pad, DIFF_HEADS, 2 * DIFF_DH)
        lam_init = 0.8 - 0.6 * math.exp(-0.3 * layer)
        lp = diff_lambda[layer].astype(jnp.float32)
        lam = (jnp.exp(jnp.sum(lp[0] * lp[1])) - jnp.exp(jnp.sum(lp[2] * lp[3])) + lam_init)
        od = causal_diff_attention(qd[..., 0, :], kd[..., 0, :], qd[..., 1, :], kd[..., 1, :],
                                   vd, lam, diff_scale)
        od = rms_norm(od, diff_subln[layer]) * (1.0 - lam_init)
        out_b = od.reshape(b, l_pad, DIFF_HEADS * 2 * DIFF_DH) @ w_o_diff[layer].astype(dt)

        mix = (jax.nn.sigmoid(g_a) * out_a + jax.nn.sigmoid(g_b) * out_b) @ w_out[layer].astype(dt)
        r = r + rms_norm(mix, norm_mix_post[layer])

        h = rms_norm(r, norm_ffn_pre[layer])
        gu = h @ w_gate_up[layer].astype(dt)
        ff = (jax.nn.silu(gu[..., :D_FF]) * gu[..., D_FF:]) @ w_down[layer].astype(dt)
        r = r + rms_norm(ff, norm_ffn_post[layer])

    return r[:, N_META:N_META + s, :]
```

```python
import functools
import math

import jax
import jax.numpy as jnp
from jax import lax
from jax.experimental import pallas as pl
from jax.experimental.pallas import tpu as pltpu

N_META = 16
Q_BLOCK = 128
ROPE_THETA = 500000.0
RMS_EPS = 1e-6

MLA_HEADS = 8
MLA_NOPE = 64
MLA_ROPE = 32
MLA_V = 64
MLA_Q_RANK = 384
MLA_KV_RANK = 256

DIFF_HEADS = 4
DIFF_DH = 64
DIFF_ROPE = DIFF_DH // 4

LANES = 128
VMEM_LIMIT_BYTES = 56 * 1024 * 1024
MASK_VALUE = -1e30

BF16 = jnp.bfloat16
F32 = jnp.float32


def _rms(x, gain):
    ms = jnp.mean(x * x, axis=-1, keepdims=True)
    return x * lax.rsqrt(ms + RMS_EPS) * gain


def _rope(x, c, sa, sb, half):
    return x * c + pltpu.roll(x, LANES - half, 1) * sa + pltpu.roll(x, half, 1) * sb


def _dot(a, b):
    return jnp.dot(a, b, preferred_element_type=F32)


def _dot_nt(a, b):
    return lax.dot_general(a, b, (((1,), (1,)), ((), ())), preferred_element_type=F32)


def _const_spec(shape):
    nd = len(shape)
    return pl.BlockSpec(shape, lambda *_: (0,) * nd, pipeline_mode=pl.Buffered(1))


def _params(n_grid):
    return pltpu.CompilerParams(
        dimension_semantics=("arbitrary",) * n_grid,
        vmem_limit_bytes=VMEM_LIMIT_BYTES,
    )


def _proj_kernel(r_ref, g_pre_ref, w1_ref, g_q_ref, wuq_ref, g_kv_ref, wkv_ref,
                 tq_ref, tk_ref, td_ref, tdk_ref,
                 q_ref, k_ref, v_ref, dq_ref, dk_ref, dv_ref):
    r = r_ref[0]
    h = _rms(r, g_pre_ref[...]).astype(BF16)
    p1 = _dot(h, w1_ref[...])
    o = 0
    q_lat = p1[:, o:o + MLA_Q_RANK]; o += MLA_Q_RANK
    kv_lat = p1[:, o:o + MLA_KV_RANK]; o += MLA_KV_RANK
    kpe = p1[:, o:o + LANES]; o += LANES
    dq = p1[:, o:o + 512]; o += 512
    dk = p1[:, o:o + 512]; o += 512
    dv = p1[:, o:o + 512]

    cq = _rms(q_lat, g_q_ref[...]).astype(BF16)
    qa = _dot(cq, wuq_ref[...])
    cqt, saq, sbq = tq_ref[0], tq_ref[1], tq_ref[2]
    for hd in range(MLA_HEADS):
        blk = qa[:, hd * LANES:(hd + 1) * LANES]
        q_ref[0, :, hd * LANES:(hd + 1) * LANES] = _rope(blk, cqt, saq, sbq, MLA_ROPE // 2).astype(BF16)

    ckv = _rms(kv_lat, g_kv_ref[...]).astype(BF16)
    kv = _dot(ckv, wkv_ref[...])
    kpe_r = _rope(kpe, tk_ref[0], tk_ref[1], tk_ref[2], MLA_ROPE // 2)
    for hd in range(MLA_HEADS):
        k_ref[0, :, hd * LANES:(hd + 1) * LANES] = (kv[:, hd * LANES:(hd + 1) * LANES] + kpe_r).astype(BF16)
    v_ref[0] = kv[:, MLA_HEADS * LANES:].astype(BF16)

    for hd in range(DIFF_HEADS):
        sl = slice(hd * LANES, (hd + 1) * LANES)
        dq_ref[0, :, sl] = _rope(dq[:, sl], td_ref[0], td_ref[1], td_ref[2], DIFF_ROPE // 2).astype(BF16)
        dk_ref[0, :, sl] = _rope(dk[:, sl], tdk_ref[0], tdk_ref[1], tdk_ref[2], DIFF_ROPE // 2).astype(BF16)
    dv_ref[0] = dv.astype(BF16)


def _proj_call(r, g_pre, w1, g_q, wuq, g_kv, wkv, tq, tk, td, tdk, tm):
    b, l, d = r.shape
    nt = l // tm
    tab_spec = pl.BlockSpec((3, tm, LANES), lambda bi, ti: (0, ti, 0))

    def out(n):
        return pl.BlockSpec((1, tm, n), lambda bi, ti: (bi, ti, 0))

    return pl.pallas_call(
        _proj_kernel,
        grid=(b, nt),
        in_specs=[
            pl.BlockSpec((1, tm, d), lambda bi, ti: (bi, ti, 0)),
            _const_spec(g_pre.shape), _const_spec(w1.shape),
            _const_spec(g_q.shape), _const_spec(wuq.shape),
            _const_spec(g_kv.shape), _const_spec(wkv.shape),
            tab_spec, tab_spec, tab_spec, tab_spec,
        ],
        out_specs=[out(1024), out(1024), out(512), out(512), out(512), out(512)],
        out_shape=[jax.ShapeDtypeStruct((b, l, n), BF16) for n in (1024, 1024, 512, 512, 512, 512)],
        compiler_params=_params(2),
        name="proj",
    )(r, g_pre, w1, g_q, wuq, g_kv, wkv, tq, tk, td, tdk)


def _softmax_step(q, k, v, carry, mask):
    m, l, acc = carry
    s = _dot_nt(q, k)
    if mask is not None:
        s = jnp.where(mask, s, MASK_VALUE)
    m_new = jnp.maximum(m, jnp.max(s, axis=-1, keepdims=True))
    alpha = jnp.exp(m - m_new)
    p = jnp.exp(s - m_new)
    l = alpha * l + jnp.sum(p, axis=-1, keepdims=True)
    acc = alpha * acc + _dot(p.astype(BF16), v)
    return m_new, l, acc


def _causal_attend(q, k_of, v_of, qi, t):
    init = (jnp.full((t, 1), MASK_VALUE, F32), jnp.zeros((t, 1), F32), jnp.zeros((t, LANES), F32))

    def body(j, carry):
        off = pl.multiple_of(j * t, t)
        return _softmax_step(q, k_of(off), v_of(off), carry, None)

    carry = lax.fori_loop(0, qi, body, init)
    off = pl.multiple_of(qi * t, t)
    row = lax.broadcasted_iota(jnp.int32, (t, t), 0)
    col = lax.broadcasted_iota(jnp.int32, (t, t), 1)
    m, l, acc = _softmax_step(q, k_of(off), v_of(off), carry, col <= row)
    return acc / l


def _mla_kernel(q_ref, k_ref, v_ref, o_ref, *, t):
    qi = pl.program_id(2)
    outs = []
    for hh in range(2):
        sl = slice(hh * LANES, (hh + 1) * LANES)
        q = q_ref[0, :, sl]
        outs.append(_causal_attend(
            q,
            lambda off, sl=sl: k_ref[0, pl.ds(off, t), sl],
            lambda off: v_ref[0, pl.ds(off, t), :],
            qi, t))
    lane = lax.broadcasted_iota(jnp.int32, (t, LANES), 1)
    o_ref[0] = jnp.where(lane < MLA_V, outs[0], outs[1]).astype(o_ref.dtype)


def _mla_call(q, k, v, t):
    b, l, _ = q.shape
    nq = l // t
    return pl.pallas_call(
        functools.partial(_mla_kernel, t=t),
        grid=(b, MLA_HEADS // 2, nq),
        in_specs=[
            pl.BlockSpec((1, t, 2 * LANES), lambda bi, hp, qi: (bi, qi, hp)),
            pl.BlockSpec((1, l, 2 * LANES), lambda bi, hp, qi: (bi, 0, hp)),
            pl.BlockSpec((1, l, LANES), lambda bi, hp, qi: (bi, 0, hp)),
        ],
        out_specs=pl.BlockSpec((1, t, LANES), lambda bi, hp, qi: (bi, qi, hp)),
        out_shape=jax.ShapeDtypeStruct((b, l, MLA_HEADS * MLA_V), BF16),
        compiler_params=_params(3),
        name="mla_attn",
    )(q, k, v)


def _diff_kernel(q_ref, k_ref, v_ref, lam_ref, g_ref, o_ref, *, t, lam_init):
    qi = pl.program_id(2)
    q = q_ref[0]
    lane = lax.broadcasted_iota(jnp.int32, (t, LANES), 1)
    zero = jnp.zeros_like(q)
    q1 = jnp.where(lane < DIFF_DH, q, zero)
    q2 = jnp.where(lane < DIFF_DH, zero, q)
    k_of = lambda off: k_ref[0, pl.ds(off, t), :]
    v_of = lambda off: v_ref[0, pl.ds(off, t), :]
    o1 = _causal_attend(q1, k_of, v_of, qi, t)
    o2 = _causal_attend(q2, k_of, v_of, qi, t)
    lp = lam_ref[...]
    lam = (jnp.exp(jnp.sum(lp[0:1] * lp[1:2], axis=-1, keepdims=True))
           - jnp.exp(jnp.sum(lp[2:3] * lp[3:4], axis=-1, keepdims=True)) + lam_init)
    od = o1 - lam * o2
    o_ref[0] = (_rms(od, g_ref[...]) * (1.0 - lam_init)).astype(o_ref.dtype)


def _diff_call(q, k, v, lam_p, g_sub, t, lam_init):
    b, l, _ = q.shape
    nq = l // t
    return pl.pallas_call(
        functools.partial(_diff_kernel, t=t, lam_init=lam_init),
        grid=(b, DIFF_HEADS, nq),
        in_specs=[
            pl.BlockSpec((1, t, LANES), lambda bi, hd, qi: (bi, qi, hd)),
            pl.BlockSpec((1, l, LANES), lambda bi, hd, qi: (bi, 0, hd)),
            pl.BlockSpec((1, l, LANES), lambda bi, hd, qi: (bi, 0, hd)),
            _const_spec(lam_p.shape), _const_spec(g_sub.shape),
        ],
        out_specs=pl.BlockSpec((1, t, LANES), lambda bi, hd, qi: (bi, qi, hd)),
        out_shape=jax.ShapeDtypeStruct((b, l, DIFF_HEADS * 2 * DIFF_DH), BF16),
        compiler_params=_params(3),
        name="diff_attn",
    )(q, k, v, lam_p, g_sub)


def _merge_kernel(r_ref, oa_ref, od_ref, g_pre_ref, wg_ref, woa_ref, wod_ref, wout_ref, g_post_ref, o_ref):
    r = r_ref[...]
    d = r.shape[-1]
    h = _rms(r, g_pre_ref[...]).astype(BF16)
    g = _dot(h, wg_ref[...])
    a = _dot(oa_ref[...], woa_ref[...])
    bb = _dot(od_ref[...], wod_ref[...])
    mixed = (jax.nn.sigmoid(g[:, :d]) * a + jax.nn.sigmoid(g[:, d:]) * bb).astype(BF16)
    mix = _dot(mixed, wout_ref[...])
    o_ref[...] = r + _rms(mix, g_post_ref[...])


def _merge_call(r2, oa2, od2, g_pre, wg, woa, wod, wout, g_post, tm):
    n, d = r2.shape
    row = lambda w: pl.BlockSpec((tm, w), lambda i: (i, 0))
    return pl.pallas_call(
        _merge_kernel,
        grid=(n // tm,),
        in_specs=[row(d), row(oa2.shape[1]), row(od2.shape[1]),
                  _const_spec(g_pre.shape), _const_spec(wg.shape), _const_spec(woa.shape),
                  _const_spec(wod.shape), _const_spec(wout.shape), _const_spec(g_post.shape)],
        out_specs=row(d),
        out_shape=jax.ShapeDtypeStruct((n, d), F32),
        compiler_params=_params(1),
        name="merge",
    )(r2, oa2, od2, g_pre, wg, woa, wod, wout, g_post)


def _ffn_kernel(r_ref, g_pre_ref, wgu_ref, wd_ref, g_post_ref, o_ref):
    r = r_ref[...]
    dff = wd_ref.shape[0]
    h = _rms(r, g_pre_ref[...]).astype(BF16)
    gu = _dot(h, wgu_ref[...])
    gate = gu[:, :dff]
    act = (gate * jax.nn.sigmoid(gate) * gu[:, dff:]).astype(BF16)
    ff = _dot(act, wd_ref[...])
    o_ref[...] = r + _rms(ff, g_post_ref[...])


def _ffn_call(r2, g_pre, wgu, wd, g_post, tm):
    n, d = r2.shape
    row = pl.BlockSpec((tm, d), lambda i: (i, 0))
    return pl.pallas_call(
        _ffn_kernel,
        grid=(n // tm,),
        in_specs=[row, _const_spec(g_pre.shape), _const_spec(wgu.shape),
                  _const_spec(wd.shape), _const_spec(g_post.shape)],
        out_specs=row,
        out_shape=jax.ShapeDtypeStruct((n, d), F32),
        compiler_params=_params(1),
        name="ffn",
    )(r2, g_pre, wgu, wd, g_post)


def _rope_tables(l_pad, rot_dim, group_starts, scale):
    half = rot_dim // 2
    inv = ROPE_THETA ** (-jnp.arange(0, rot_dim, 2, dtype=F32) / rot_dim)
    ang = jnp.arange(l_pad, dtype=F32)[:, None] * inv[None, :]
    cos, sin = jnp.cos(ang), jnp.sin(ang)
    c = jnp.ones((l_pad, LANES), F32)
    sa = jnp.zeros((l_pad, LANES), F32)
    sb = jnp.zeros((l_pad, LANES), F32)
    for g in group_starts:
        c = c.at[:, g:g + half].set(cos).at[:, g + half:g + rot_dim].set(cos)
        sa = sa.at[:, g:g + half].set(-sin)
        sb = sb.at[:, g + half:g + rot_dim].set(sin)
    return jnp.stack([c, sa, sb]) * scale


def _pick_tile(n, candidates):
    for c in candidates:
        if n % c == 0:
            return c
    raise ValueError(f"no tile in {candidates} divides {n}")


def kernel(x, meta_tokens, norm_mix_pre, norm_mix_post, norm_ffn_pre, norm_ffn_post, w_in, mla_q_norm, mla_kv_norm, mla_w_uq, mla_w_ukv, w_o_mla, diff_lambda, diff_subln, w_o_diff, w_out, w_gate_up, w_down):
    dt = x.dtype
    b, s, d = x.shape
    depth = w_in.shape[0]
    length = s + N_META
    l_pad = -(-length // Q_BLOCK) * Q_BLOCK
    meta = jnp.broadcast_to(meta_tokens.astype(dt)[None], (b, N_META, d))
    r = jnp.concatenate([meta, x, jnp.zeros((b, l_pad - length, d), dt)], axis=1)

    t_attn = _pick_tile(l_pad, (384, 128))
    t_proj = _pick_tile(l_pad, (384, 128))
    t_tok = _pick_tile(b * l_pad, (512, 384, 256, 128))

    mla_scale = 1.0 / math.sqrt(MLA_NOPE + MLA_ROPE)
    diff_scale = 1.0 / math.sqrt(DIFF_DH)
    tab_q = _rope_tables(l_pad, MLA_ROPE, (MLA_NOPE,), mla_scale)
    tab_k = _rope_tables(l_pad, MLA_ROPE, (MLA_NOPE,), 1.0)
    tab_dq = _rope_tables(l_pad, DIFF_ROPE, (0, DIFF_DH), diff_scale)
    tab_dk = _rope_tables(l_pad, DIFF_ROPE, (0, DIFF_DH), 1.0)

    c0 = MLA_Q_RANK
    c1 = c0 + MLA_KV_RANK
    c2 = c1 + MLA_ROPE
    c3 = c2 + 3 * DIFF_HEADS * 2 * DIFF_DH
    row1 = lambda v: v.reshape(1, -1).astype(F32)

    for layer in range(depth):
        wl = w_in[layer]
        kpe_w = jnp.zeros((d, LANES), dt).at[:, MLA_NOPE:MLA_NOPE + MLA_ROPE].set(wl[:, c1:c2])
        w1 = jnp.concatenate([wl[:, :c1], kpe_w, wl[:, c2:c3]], axis=1).astype(BF16)
        wg = wl[:, c3:].astype(BF16)
        wuq = jnp.pad(mla_w_uq[layer].reshape(MLA_Q_RANK, MLA_HEADS, MLA_NOPE + MLA_ROPE),
                      ((0, 0), (0, 0), (0, LANES - MLA_NOPE - MLA_ROPE))
                      ).reshape(MLA_Q_RANK, MLA_HEADS * LANES).astype(BF16)
        wukv = mla_w_ukv[layer].reshape(MLA_KV_RANK, MLA_HEADS, MLA_NOPE + MLA_V)
        wk = jnp.pad(wukv[:, :, :MLA_NOPE], ((0, 0), (0, 0), (0, LANES - MLA_NOPE))
                     ).reshape(MLA_KV_RANK, MLA_HEADS * LANES)
        wv = wukv[:, :, MLA_NOPE:].reshape(MLA_KV_RANK, MLA_HEADS * MLA_V)
        wkv = jnp.concatenate([wk, wv], axis=1).astype(BF16)

        q, k, v, dq, dk, dv = _proj_call(
            r, row1(norm_mix_pre[layer]), w1, row1(mla_q_norm[layer]), wuq,
            row1(mla_kv_norm[layer]), wkv, tab_q, tab_k, tab_dq, tab_dk, t_proj)

        out_a = _mla_call(q, k, v, t_attn)
        lam_init = 0.8 - 0.6 * math.exp(-0.3 * layer)
        out_d = _diff_call(dq, dk, dv, diff_lambda[layer].astype(F32), row1(diff_subln[layer]),
                           t_attn, lam_init)

        r2 = r.reshape(b * l_pad, d)
        r2 = _merge_call(r2, out_a.reshape(b * l_pad, -1), out_d.reshape(b * l_pad, -1),
                         row1(norm_mix_pre[layer]), wg, w_o_mla[layer].astype(BF16),
                         w_o_diff[layer].astype(BF16), w_out[layer].astype(BF16),
                         row1(norm_mix_post[layer]), t_tok)
        r2 = _ffn_call(r2, row1(norm_ffn_pre[layer]), w_gate_up[layer].astype(BF16),
                       w_down[layer].astype(BF16), row1(norm_ffn_post[layer]), t_tok)
        r = r2.reshape(b, l_pad, d)

    return r[:, N_META:N_META + s, :]
```

```python
import functools
import math

import jax
import jax.numpy as jnp
from jax import lax
from jax.experimental import pallas as pl
from jax.experimental.pallas import tpu as pltpu

N_META = 16
Q_BLOCK = 128
ROPE_THETA = 500000.0
RMS_EPS = 1e-6

MLA_HEADS = 8
MLA_NOPE = 64
MLA_ROPE = 32
MLA_V = 64
MLA_Q_RANK = 384
MLA_KV_RANK = 256

DIFF_HEADS = 4
DIFF_DH = 64
DIFF_ROPE = DIFF_DH // 4
DIFF_V = 2 * DIFF_DH

LANES = 128
MLA_W = MLA_HEADS * LANES
DIFF_W = DIFF_HEADS * DIFF_V
MLA_HEADS_PER_STEP = 4
DIFF_HEADS_PER_STEP = 2
VMEM_LIMIT_BYTES = 56 * 1024 * 1024
MASK_VALUE = -1e30
LOG2E = 1.4426950408889634

BF16 = jnp.bfloat16
F32 = jnp.float32


def _rms(x, gain):
    ms = jnp.mean(x * x, axis=-1, keepdims=True)
    return x * lax.rsqrt(ms + RMS_EPS) * gain


def _rope(x, c, sa, sb, half):
    return x * c + pltpu.roll(x, LANES - half, 1) * sa + pltpu.roll(x, half, 1) * sb


def _dot(a, b):
    return jnp.dot(a, b, preferred_element_type=F32)


def _dot_nt(a, b):
    return lax.dot_general(a, b, (((1,), (1,)), ((), ())), preferred_element_type=F32)


def _const_spec(shape):
    nd = len(shape)
    return pl.BlockSpec(shape, lambda *_: (0,) * nd, pipeline_mode=pl.Buffered(1))


def _params(n_grid):
    return pltpu.CompilerParams(
        dimension_semantics=("arbitrary",) * n_grid,
        vmem_limit_bytes=VMEM_LIMIT_BYTES,
    )


def _proj_kernel(r_ref, g_pre_ref, w1_ref, wdvt_ref, g_q_ref, wuq_ref, g_kv_ref, wk_ref, wvt_ref,
                 tq_ref, tk_ref, td_ref, tdk_ref,
                 q_ref, k_ref, vt_ref, dq_ref, dk_ref, dvt_ref):
    r = r_ref[0]
    h = _rms(r, g_pre_ref[...]).astype(BF16)
    p1 = _dot(h, w1_ref[...])
    o = 0
    q_lat = p1[:, o:o + MLA_Q_RANK]; o += MLA_Q_RANK
    kv_lat = p1[:, o:o + MLA_KV_RANK]; o += MLA_KV_RANK
    kpe = p1[:, o:o + LANES]; o += LANES
    dq = p1[:, o:o + DIFF_W]; o += DIFF_W
    dk = p1[:, o:o + DIFF_W]

    cq = _rms(q_lat, g_q_ref[...]).astype(BF16)
    qa = _dot(cq, wuq_ref[...])
    cqt, saq, sbq = tq_ref[0], tq_ref[1], tq_ref[2]
    for hd in range(MLA_HEADS):
        sl = slice(hd * LANES, (hd + 1) * LANES)
        q_ref[0, :, sl] = _rope(qa[:, sl], cqt, saq, sbq, MLA_ROPE // 2).astype(BF16)

    ckv = _rms(kv_lat, g_kv_ref[...]).astype(BF16)
    kn = _dot(ckv, wk_ref[...])
    kpe_r = _rope(kpe, tk_ref[0], tk_ref[1], tk_ref[2], MLA_ROPE // 2)
    for hd in range(MLA_HEADS):
        sl = slice(hd * LANES, (hd + 1) * LANES)
        k_ref[0, :, sl] = (kn[:, sl] + kpe_r).astype(BF16)
    vt_ref[0, 0] = _dot_nt(wvt_ref[...], ckv).astype(BF16)

    for hd in range(DIFF_HEADS):
        sl = slice(hd * LANES, (hd + 1) * LANES)
        dq_ref[0, :, sl] = _rope(dq[:, sl], td_ref[0], td_ref[1], td_ref[2], DIFF_ROPE // 2).astype(BF16)
        dk_ref[0, :, sl] = _rope(dk[:, sl], tdk_ref[0], tdk_ref[1], tdk_ref[2], DIFF_ROPE // 2).astype(BF16)
    dvt_ref[0, 0] = _dot_nt(wdvt_ref[...], h).astype(BF16)


def _proj_call(r, g_pre, w1, wdvt, g_q, wuq, g_kv, wk, wvt, tq, tk, td, tdk, tm):
    b, l, d = r.shape
    nt = l // tm
    tab_spec = pl.BlockSpec((3, tm, LANES), lambda bi, ti: (0, ti, 0))

    def rows(n):
        return pl.BlockSpec((1, tm, n), lambda bi, ti: (bi, ti, 0))

    def cols(n):
        return pl.BlockSpec((1, 1, n, tm), lambda bi, ti: (bi, ti, 0, 0))

    nv, ndv = wvt.shape[0], wdvt.shape[0]
    return pl.pallas_call(
        _proj_kernel,
        grid=(b, nt),
        in_specs=[
            pl.BlockSpec((1, tm, d), lambda bi, ti: (bi, ti, 0)),
            _const_spec(g_pre.shape), _const_spec(w1.shape), _const_spec(wdvt.shape),
            _const_spec(g_q.shape), _const_spec(wuq.shape),
            _const_spec(g_kv.shape), _const_spec(wk.shape), _const_spec(wvt.shape),
            tab_spec, tab_spec, tab_spec, tab_spec,
        ],
        out_specs=[rows(MLA_W), rows(MLA_W), cols(nv), rows(DIFF_W), rows(DIFF_W), cols(ndv)],
        out_shape=[
            jax.ShapeDtypeStruct((b, l, MLA_W), BF16),
            jax.ShapeDtypeStruct((b, l, MLA_W), BF16),
            jax.ShapeDtypeStruct((b, nt, nv, tm), BF16),
            jax.ShapeDtypeStruct((b, l, DIFF_W), BF16),
            jax.ShapeDtypeStruct((b, l, DIFF_W), BF16),
            jax.ShapeDtypeStruct((b, nt, ndv, tm), BF16),
        ],
        compiler_params=_params(2),
        name="proj",
    )(r, g_pre, w1, wdvt, g_q, wuq, g_kv, wk, wvt, tq, tk, td, tdk)


def _causal_attend(qs, k_of, vt_of, qi, t, acc_ref, s_ref):
    n = len(qs)
    acc_ref[...] = jnp.zeros_like(acc_ref)
    m0 = jnp.full((1, t), MASK_VALUE, F32)
    l0 = jnp.zeros((1, t), F32)
    key = lax.broadcasted_iota(jnp.int32, (t, t), 0)
    qry = lax.broadcasted_iota(jnp.int32, (t, t), 1)

    def matmul_scores(j):
        ks = k_of(j)
        return [_dot_nt(ks[c], qs[c]) for c in range(n)]

    def park_scores(ss, mask):
        tile_max = []
        for c in range(n):
            s = ss[c] if mask is None else jnp.where(mask, ss[c], MASK_VALUE)
            s_ref[c] = s
            tile_max.append(jnp.max(s, axis=0, keepdims=True))
        return tuple(tile_max)

    def update(j, tile_max, ms, ls):
        vts = vt_of(j)
        ms_new, ls_new = [], []
        for c in range(n):
            m_new = jnp.maximum(ms[c], tile_max[c])
            alpha = jnp.exp2(ms[c] - m_new)
            p = jnp.exp2(s_ref[c] - m_new)
            ls_new.append(alpha * ls[c] + jnp.sum(p, axis=0, keepdims=True))
            acc_ref[c] = alpha * acc_ref[c] + _dot(vts[c], p.astype(BF16))
            ms_new.append(m_new)
        return tuple(ms_new), tuple(ls_new)

    def step(j, carry, mask):
        tile_max, ms, ls = carry
        ss = matmul_scores(j + 1)
        ms, ls = update(j, tile_max, ms, ls)
        return park_scores(ss, mask), ms, ls

    def body(j, carry):
        return step(j, carry, None)

    def peeled(carry):
        return step(qi - 1, carry, key <= qry)

    carry = (park_scores(matmul_scores(0), key <= qry + qi * t), (m0,) * n, (l0,) * n)
    carry = lax.fori_loop(0, qi - 1, body, carry)
    carry = lax.cond(qi >= 1, peeled, lambda cr: cr, carry)
    _, ls = update(qi, *carry)
    return ls


def _mla_kernel(q_ref, k_ref, vt_ref, o_ref, acc_ref, s_ref, *, t, heads):
    qi = pl.program_id(2)
    lane = [slice(c * LANES, (c + 1) * LANES) for c in range(heads)]
    rows = [slice(c * MLA_V, (c + 1) * MLA_V) for c in range(heads)]
    qs = [q_ref[0, :, lane[c]] for c in range(heads)]

    def k_of(j):
        off = pl.multiple_of(j * t, t)
        return [k_ref[0, pl.ds(off, t), lane[c]] for c in range(heads)]

    def vt_of(j):
        return [vt_ref[0, j, rows[c], :] for c in range(heads)]

    ls = _causal_attend(qs, k_of, vt_of, qi, t, acc_ref, s_ref)
    out_t = jnp.concatenate([acc_ref[c] / ls[c] for c in range(heads)], axis=0)
    o_ref[0] = out_t.T.astype(o_ref.dtype)


def _mla_call(q, k, vt, t):
    b, l, _ = q.shape
    nq = l // t
    hs = MLA_HEADS_PER_STEP
    return pl.pallas_call(
        functools.partial(_mla_kernel, t=t, heads=hs),
        grid=(b, MLA_HEADS // hs, nq),
        in_specs=[
            pl.BlockSpec((1, t, hs * LANES), lambda bi, hg, qi: (bi, qi, hg)),
            pl.BlockSpec((1, l, hs * LANES), lambda bi, hg, qi: (bi, 0, hg)),
            pl.BlockSpec((1, nq, hs * MLA_V, t), lambda bi, hg, qi: (bi, 0, hg, 0)),
        ],
        out_specs=pl.BlockSpec((1, t, hs * MLA_V), lambda bi, hg, qi: (bi, qi, hg)),
        out_shape=jax.ShapeDtypeStruct((b, l, MLA_HEADS * MLA_V), BF16),
        scratch_shapes=[pltpu.VMEM((hs, MLA_V, t), F32), pltpu.VMEM((hs, t, t), F32)],
        compiler_params=_params(3),
        name="mla_attn",
    )(q, k, vt)


def _diff_kernel(q_ref, k_ref, vt_ref, lam_ref, g_ref, o_ref, acc_ref, s_ref, *, t, heads, lam_init):
    qi = pl.program_id(2)
    lane = lax.broadcasted_iota(jnp.int32, (t, LANES), 1)
    qs = []
    for hd in range(heads):
        q = q_ref[0, :, hd * LANES:(hd + 1) * LANES]
        zero = jnp.zeros_like(q)
        qs += [jnp.where(lane < DIFF_DH, q, zero), jnp.where(lane < DIFF_DH, zero, q)]

    def k_of(j):
        off = pl.multiple_of(j * t, t)
        ks = [k_ref[0, pl.ds(off, t), hd * LANES:(hd + 1) * LANES] for hd in range(heads)]
        return [ks[c // 2] for c in range(2 * heads)]

    def vt_of(j):
        vts = [vt_ref[0, j, hd * DIFF_V:(hd + 1) * DIFF_V, :] for hd in range(heads)]
        return [vts[c // 2] for c in range(2 * heads)]

    ls = _causal_attend(qs, k_of, vt_of, qi, t, acc_ref, s_ref)
    lp = lam_ref[...]
    lam = (jnp.exp(jnp.sum(lp[0:1] * lp[1:2], axis=-1, keepdims=True))
           - jnp.exp(jnp.sum(lp[2:3] * lp[3:4], axis=-1, keepdims=True)) + lam_init)
    for hd in range(heads):
        o1 = acc_ref[2 * hd] / ls[2 * hd]
        o2 = acc_ref[2 * hd + 1] / ls[2 * hd + 1]
        od = (o1 - lam * o2).T
        o_ref[0, :, hd * DIFF_V:(hd + 1) * DIFF_V] = (
            _rms(od, g_ref[...]) * (1.0 - lam_init)).astype(o_ref.dtype)


def _diff_call(q, k, vt, lam_p, g_sub, t, lam_init):
    b, l, _ = q.shape
    nq = l // t
    hs = DIFF_HEADS_PER_STEP
    return pl.pallas_call(
        functools.partial(_diff_kernel, t=t, heads=hs, lam_init=lam_init),
        grid=(b, DIFF_HEADS // hs, nq),
        in_specs=[
            pl.BlockSpec((1, t, hs * LANES), lambda bi, hg, qi: (bi, qi, hg)),
            pl.BlockSpec((1, l, hs * LANES), lambda bi, hg, qi: (bi, 0, hg)),
            pl.BlockSpec((1, nq, hs * DIFF_V, t), lambda bi, hg, qi: (bi, 0, hg, 0)),
            _const_spec(lam_p.shape), _const_spec(g_sub.shape),
        ],
        out_specs=pl.BlockSpec((1, t, hs * DIFF_V), lambda bi, hg, qi: (bi, qi, hg)),
        out_shape=jax.ShapeDtypeStruct((b, l, DIFF_W), BF16),
        scratch_shapes=[pltpu.VMEM((2 * hs, DIFF_V, t), F32), pltpu.VMEM((2 * hs, t, t), F32)],
        compiler_params=_params(3),
        name="diff_attn",
    )(q, k, vt, lam_p, g_sub)


def _merge_kernel(r_ref, oa_ref, od_ref, g_pre_ref, wg_ref, woa_ref, wod_ref, wout_ref, g_post_ref, o_ref):
    r = r_ref[...]
    d = r.shape[-1]
    h = _rms(r, g_pre_ref[...]).astype(BF16)
    g = _dot(h, wg_ref[...])
    a = _dot(oa_ref[...], woa_ref[...])
    bb = _dot(od_ref[...], wod_ref[...])
    mixed = (jax.nn.sigmoid(g[:, :d]) * a + jax.nn.sigmoid(g[:, d:]) * bb).astype(BF16)
    mix = _dot(mixed, wout_ref[...])
    o_ref[...] = r + _rms(mix, g_post_ref[...])


def _merge_call(r2, oa2, od2, g_pre, wg, woa, wod, wout, g_post, tm):
    n, d = r2.shape
    row = lambda w: pl.BlockSpec((tm, w), lambda i: (i, 0))
    return pl.pallas_call(
        _merge_kernel,
        grid=(n // tm,),
        in_specs=[row(d), row(oa2.shape[1]), row(od2.shape[1]),
                  _const_spec(g_pre.shape), _const_spec(wg.shape), _const_spec(woa.shape),
                  _const_spec(wod.shape), _const_spec(wout.shape), _const_spec(g_post.shape)],
        out_specs=row(d),
        out_shape=jax.ShapeDtypeStruct((n, d), F32),
        compiler_params=_params(1),
        name="merge",
    )(r2, oa2, od2, g_pre, wg, woa, wod, wout, g_post)


def _ffn_kernel(r_ref, g_pre_ref, wgu_ref, wd_ref, g_post_ref, o_ref):
    r = r_ref[...]
    dff = wd_ref.shape[0]
    h = _rms(r, g_pre_ref[...]).astype(BF16)
    gu = _dot(h, wgu_ref[...])
    gate = gu[:, :dff]
    act = (gate * jax.nn.sigmoid(gate) * gu[:, dff:]).astype(BF16)
    ff = _dot(act, wd_ref[...])
    o_ref[...] = r + _rms(ff, g_post_ref[...])


def _ffn_call(r2, g_pre, wgu, wd, g_post, tm):
    n, d = r2.shape
    row = pl.BlockSpec((tm, d), lambda i: (i, 0))
    return pl.pallas_call(
        _ffn_kernel,
        grid=(n // tm,),
        in_specs=[row, _const_spec(g_pre.shape), _const_spec(wgu.shape),
                  _const_spec(wd.shape), _const_spec(g_post.shape)],
        out_specs=row,
        out_shape=jax.ShapeDtypeStruct((n, d), F32),
        compiler_params=_params(1),
        name="ffn",
    )(r2, g_pre, wgu, wd, g_post)


def _rope_tables(l_pad, rot_dim, group_starts, scale):
    half = rot_dim // 2
    inv = ROPE_THETA ** (-jnp.arange(0, rot_dim, 2, dtype=F32) / rot_dim)
    ang = jnp.arange(l_pad, dtype=F32)[:, None] * inv[None, :]
    cos, sin = jnp.cos(ang), jnp.sin(ang)
    c = jnp.ones((l_pad, LANES), F32)
    sa = jnp.zeros((l_pad, LANES), F32)
    sb = jnp.zeros((l_pad, LANES), F32)
    for g in group_starts:
        c = c.at[:, g:g + half].set(cos).at[:, g + half:g + rot_dim].set(cos)
        sa = sa.at[:, g:g + half].set(-sin)
        sb = sb.at[:, g + half:g + rot_dim].set(sin)
    return jnp.stack([c, sa, sb]) * scale


def _pick_tile(n, candidates):
    for c in candidates:
        if n % c == 0:
            return c
    raise ValueError(f"no tile in {candidates} divides {n}")


def kernel(x, meta_tokens, norm_mix_pre, norm_mix_post, norm_ffn_pre, norm_ffn_post, w_in, mla_q_norm, mla_kv_norm, mla_w_uq, mla_w_ukv, w_o_mla, diff_lambda, diff_subln, w_o_diff, w_out, w_gate_up, w_down):
    dt = x.dtype
    b, s, d = x.shape
    depth = w_in.shape[0]
    length = s + N_META
    l_pad = -(-length // Q_BLOCK) * Q_BLOCK
    meta = jnp.broadcast_to(meta_tokens.astype(dt)[None], (b, N_META, d))
    r = jnp.concatenate([meta, x, jnp.zeros((b, l_pad - length, d), dt)], axis=1)

    t_seq = _pick_tile(l_pad, (384, 128))
    t_tok = _pick_tile(b * l_pad, (512, 384, 256, 128))

    mla_scale = LOG2E / math.sqrt(MLA_NOPE + MLA_ROPE)
    diff_scale = LOG2E / math.sqrt(DIFF_DH)
    tab_q = _rope_tables(l_pad, MLA_ROPE, (MLA_NOPE,), mla_scale)
    tab_k = _rope_tables(l_pad, MLA_ROPE, (MLA_NOPE,), 1.0)
    tab_dq = _rope_tables(l_pad, DIFF_ROPE, (0, DIFF_DH), diff_scale)
    tab_dk = _rope_tables(l_pad, DIFF_ROPE, (0, DIFF_DH), 1.0)

    c0 = MLA_Q_RANK
    c1 = c0 + MLA_KV_RANK
    c2 = c1 + MLA_ROPE
    c_dv = c2 + 2 * DIFF_W
    c3 = c_dv + DIFF_W
    row1 = lambda v: v.reshape(1, -1).astype(F32)

    for layer in range(depth):
        wl = w_in[layer]
        kpe_w = jnp.zeros((d, LANES), dt).at[:, MLA_NOPE:MLA_NOPE + MLA_ROPE].set(wl[:, c1:c2])
        w1 = jnp.concatenate([wl[:, :c1], kpe_w, wl[:, c2:c_dv]], axis=1).astype(BF16)
        wdvt = wl[:, c_dv:c3].T.astype(BF16)
        wg = wl[:, c3:].astype(BF16)
        wuq = jnp.pad(mla_w_uq[layer].reshape(MLA_Q_RANK, MLA_HEADS, MLA_NOPE + MLA_ROPE),
                      ((0, 0), (0, 0), (0, LANES - MLA_NOPE - MLA_ROPE))
                      ).reshape(MLA_Q_RANK, MLA_W).astype(BF16)
        wukv = mla_w_ukv[layer].reshape(MLA_KV_RANK, MLA_HEADS, MLA_NOPE + MLA_V)
        wk = jnp.pad(wukv[:, :, :MLA_NOPE], ((0, 0), (0, 0), (0, LANES - MLA_NOPE))
                     ).reshape(MLA_KV_RANK, MLA_W).astype(BF16)
        wvt = wukv[:, :, MLA_NOPE:].reshape(MLA_KV_RANK, MLA_HEADS * MLA_V).T.astype(BF16)

        q, k, vt, dq, dk, dvt = _proj_call(
            r, row1(norm_mix_pre[layer]), w1, wdvt, row1(mla_q_norm[layer]), wuq,
            row1(mla_kv_norm[layer]), wk, wvt, tab_q, tab_k, tab_dq, tab_dk, t_seq)

        out_a = _mla_call(q, k, vt, t_seq)
        lam_init = 0.8 - 0.6 * math.exp(-0.3 * layer)
        out_d = _diff_call(dq, dk, dvt, diff_lambda[layer].astype(F32), row1(diff_subln[layer]),
                           t_seq, lam_init)

        r2 = r.reshape(b * l_pad, d)
        r2 = _merge_call(r2, out_a.reshape(b * l_pad, -1), out_d.reshape(b * l_pad, -1),
                         row1(norm_mix_pre[layer]), wg, w_o_mla[layer].astype(BF16),
                         w_o_diff[layer].astype(BF16), w_out[layer].astype(BF16),
                         row1(norm_mix_post[layer]), t_tok)
        r2 = _ffn_call(r2, row1(norm_ffn_pre[layer]), w_gate_up[layer].astype(BF16),
                       w_down[layer].astype(BF16), row1(norm_ffn_post[layer]), t_tok)
        r = r2.reshape(b, l_pad, d)

    return r[:, N_META:N_META + s, :]
```

```python
import functools
import math

import jax
import jax.numpy as jnp
from jax import lax
from jax.experimental import pallas as pl
from jax.experimental.pallas import tpu as pltpu

N_META = 16
Q_BLOCK = 128
ROPE_THETA = 500000.0
RMS_EPS = 1e-6

MLA_HEADS = 8
MLA_NOPE = 64
MLA_ROPE = 32
MLA_V = 64
MLA_Q_RANK = 384
MLA_KV_RANK = 256

DIFF_HEADS = 4
DIFF_DH = 64
DIFF_ROPE = DIFF_DH // 4
DIFF_V = 2 * DIFF_DH

LANES = 128
MLA_W = MLA_HEADS * LANES
DIFF_W = DIFF_HEADS * DIFF_V
MLA_HEADS_PER_STEP = 4
DIFF_HEADS_PER_STEP = 2
VMEM_LIMIT_BYTES = 56 * 1024 * 1024
MASK_VALUE = -1e30
LOG2E = 1.4426950408889634

BF16 = jnp.bfloat16
F32 = jnp.float32


def _rms(x, gain):
    ms = jnp.mean(x * x, axis=-1, keepdims=True)
    return x * lax.rsqrt(ms + RMS_EPS) * gain


def _rope(x, c, sa, sb, half):
    return x * c + pltpu.roll(x, LANES - half, 1) * sa + pltpu.roll(x, half, 1) * sb


def _dot(a, b):
    return jnp.dot(a, b, preferred_element_type=F32)


def _dot_nt(a, b):
    return lax.dot_general(a, b, (((1,), (1,)), ((), ())), preferred_element_type=F32)


def _const_spec(shape):
    nd = len(shape)
    return pl.BlockSpec(shape, lambda *_: (0,) * nd, pipeline_mode=pl.Buffered(1))


def _params(n_grid):
    return pltpu.CompilerParams(
        dimension_semantics=("arbitrary",) * n_grid,
        vmem_limit_bytes=VMEM_LIMIT_BYTES,
    )


def _proj_kernel(r_ref, g_pre_ref, w1_ref, wdvt_ref, g_q_ref, wuq_ref, g_kv_ref, wk_ref, wvt_ref,
                 tq_ref, tk_ref, td_ref, tdk_ref,
                 q_ref, k_ref, vt_ref, dq_ref, dk_ref, dvt_ref):
    r = r_ref[0]
    h = _rms(r, g_pre_ref[...]).astype(BF16)
    p1 = _dot(h, w1_ref[...])
    o = 0
    q_lat = p1[:, o:o + MLA_Q_RANK]; o += MLA_Q_RANK
    kv_lat = p1[:, o:o + MLA_KV_RANK]; o += MLA_KV_RANK
    kpe = p1[:, o:o + LANES]; o += LANES
    dq = p1[:, o:o + DIFF_W]; o += DIFF_W
    dk = p1[:, o:o + DIFF_W]

    cq = _rms(q_lat, g_q_ref[...]).astype(BF16)
    qa = _dot(cq, wuq_ref[...])
    cqt, saq, sbq = tq_ref[0], tq_ref[1], tq_ref[2]
    for hd in range(MLA_HEADS):
        sl = slice(hd * LANES, (hd + 1) * LANES)
        q_ref[0, :, sl] = _rope(qa[:, sl], cqt, saq, sbq, MLA_ROPE // 2).astype(BF16)

    ckv = _rms(kv_lat, g_kv_ref[...]).astype(BF16)
    kn = _dot(ckv, wk_ref[...])
    kpe_r = _rope(kpe, tk_ref[0], tk_ref[1], tk_ref[2], MLA_ROPE // 2)
    for hd in range(MLA_HEADS):
        sl = slice(hd * LANES, (hd + 1) * LANES)
        k_ref[0, :, sl] = (kn[:, sl] + kpe_r).astype(BF16)
    vt_ref[0, 0] = _dot_nt(wvt_ref[...], ckv).astype(BF16)

    for hd in range(DIFF_HEADS):
        sl = slice(hd * LANES, (hd + 1) * LANES)
        dq_ref[0, :, sl] = _rope(dq[:, sl], td_ref[0], td_ref[1], td_ref[2], DIFF_ROPE // 2).astype(BF16)
        dk_ref[0, :, sl] = _rope(dk[:, sl], tdk_ref[0], tdk_ref[1], tdk_ref[2], DIFF_ROPE // 2).astype(BF16)
    dvt_ref[0, 0] = _dot_nt(wdvt_ref[...], h).astype(BF16)


def _proj_call(r, g_pre, w1, wdvt, g_q, wuq, g_kv, wk, wvt, tq, tk, td, tdk, tm):
    b, l, d = r.shape
    nt = l // tm
    tab_spec = pl.BlockSpec((3, tm, LANES), lambda bi, ti: (0, ti, 0))

    def rows(n):
        return pl.BlockSpec((1, tm, n), lambda bi, ti: (bi, ti, 0))

    def cols(n):
        return pl.BlockSpec((1, 1, n, tm), lambda bi, ti: (bi, ti, 0, 0))

    nv, ndv = wvt.shape[0], wdvt.shape[0]
    return pl.pallas_call(
        _proj_kernel,
        grid=(b, nt),
        in_specs=[
            pl.BlockSpec((1, tm, d), lambda bi, ti: (bi, ti, 0)),
            _const_spec(g_pre.shape), _const_spec(w1.shape), _const_spec(wdvt.shape),
            _const_spec(g_q.shape), _const_spec(wuq.shape),
            _const_spec(g_kv.shape), _const_spec(wk.shape), _const_spec(wvt.shape),
            tab_spec, tab_spec, tab_spec, tab_spec,
        ],
        out_specs=[rows(MLA_W), rows(MLA_W), cols(nv), rows(DIFF_W), rows(DIFF_W), cols(ndv)],
        out_shape=[
            jax.ShapeDtypeStruct((b, l, MLA_W), BF16),
            jax.ShapeDtypeStruct((b, l, MLA_W), BF16),
            jax.ShapeDtypeStruct((b, nt, nv, tm), BF16),
            jax.ShapeDtypeStruct((b, l, DIFF_W), BF16),
            jax.ShapeDtypeStruct((b, l, DIFF_W), BF16),
            jax.ShapeDtypeStruct((b, nt, ndv, tm), BF16),
        ],
        compiler_params=_params(2),
        name="proj",
    )(r, g_pre, w1, wdvt, g_q, wuq, g_kv, wk, wvt, tq, tk, td, tdk)


def _causal_attend(qs, k_of, vt_of, qi, t, acc_ref, s_ref):
    n = len(qs)
    acc_ref[...] = jnp.zeros_like(acc_ref)
    m0 = jnp.full((1, t), MASK_VALUE, F32)
    l0 = jnp.zeros((1, t), F32)
    key = lax.broadcasted_iota(jnp.int32, (t, t), 0)
    qry = lax.broadcasted_iota(jnp.int32, (t, t), 1)

    def matmul_scores(j):
        ks = k_of(j)
        return [_dot_nt(ks[c], qs[c]) for c in range(n)]

    def park_scores(ss, slot, mask):
        tile_max = []
        for c in range(n):
            s = ss[c] if mask is None else jnp.where(mask, ss[c], MASK_VALUE)
            s_ref[slot, c] = s
            tile_max.append(jnp.max(s, axis=0, keepdims=True))
        return tuple(tile_max)

    def update(j, slot, tile_max, ms, ls):
        vts = vt_of(j)
        ms_new, ls_new = [], []
        for c in range(n):
            m_new = jnp.maximum(ms[c], tile_max[c])
            alpha = jnp.exp2(ms[c] - m_new)
            p = jnp.exp2(s_ref[slot, c] - m_new)
            ls_new.append(alpha * ls[c] + jnp.sum(p, axis=0, keepdims=True))
            acc_ref[c] = alpha * acc_ref[c] + _dot(vts[c], p.astype(BF16))
            ms_new.append(m_new)
        return tuple(ms_new), tuple(ls_new)

    def step(j, slot, carry, mask):
        tile_max, ms, ls = carry
        nxt = park_scores(matmul_scores(j + 1), 1 - slot, mask)
        ms, ls = update(j, slot, tile_max, ms, ls)
        return nxt, ms, ls

    def pair(i, carry):
        carry = step(2 * i, 0, carry, None)
        return step(2 * i + 1, 1, carry, None)

    diag = key <= qry

    def finish_even(carry):
        carry = step(qi - 1, 0, carry, diag)
        return update(qi, 1, *carry)[1]

    def finish_odd(carry):
        carry = step(qi - 2, 0, carry, None)
        carry = step(qi - 1, 1, carry, diag)
        return update(qi, 0, *carry)[1]

    def finish(carry):
        return lax.cond(((qi - 1) & 1) == 0, finish_even, finish_odd, carry)

    def only_diagonal(carry):
        return update(0, 0, *carry)[1]

    carry = (park_scores(matmul_scores(0), 0, key <= qry + qi * t), (m0,) * n, (l0,) * n)
    carry = lax.fori_loop(0, jnp.maximum(qi - 1, 0) // 2, pair, carry)
    return lax.cond(qi >= 1, finish, only_diagonal, carry)


def _mla_kernel(q_ref, k_ref, vt_ref, o_ref, acc_ref, s_ref, *, t, heads):
    qi = pl.program_id(2)
    lane = [slice(c * LANES, (c + 1) * LANES) for c in range(heads)]
    rows = [slice(c * MLA_V, (c + 1) * MLA_V) for c in range(heads)]
    qs = [q_ref[0, :, lane[c]] for c in range(heads)]

    def k_of(j):
        off = pl.multiple_of(j * t, t)
        return [k_ref[0, pl.ds(off, t), lane[c]] for c in range(heads)]

    def vt_of(j):
        return [vt_ref[0, j, rows[c], :] for c in range(heads)]

    ls = _causal_attend(qs, k_of, vt_of, qi, t, acc_ref, s_ref)
    out_t = jnp.concatenate([acc_ref[c] / ls[c] for c in range(heads)], axis=0)
    o_ref[0] = out_t.T.astype(o_ref.dtype)


def _mla_call(q, k, vt, t):
    b, l, _ = q.shape
    nq = l // t
    hs = MLA_HEADS_PER_STEP
    return pl.pallas_call(
        functools.partial(_mla_kernel, t=t, heads=hs),
        grid=(b, MLA_HEADS // hs, nq),
        in_specs=[
            pl.BlockSpec((1, t, hs * LANES), lambda bi, hg, qi: (bi, qi, hg)),
            pl.BlockSpec((1, l, hs * LANES), lambda bi, hg, qi: (bi, 0, hg)),
            pl.BlockSpec((1, nq, hs * MLA_V, t), lambda bi, hg, qi: (bi, 0, hg, 0)),
        ],
        out_specs=pl.BlockSpec((1, t, hs * MLA_V), lambda bi, hg, qi: (bi, qi, hg)),
        out_shape=jax.ShapeDtypeStruct((b, l, MLA_HEADS * MLA_V), BF16),
        scratch_shapes=[pltpu.VMEM((hs, MLA_V, t), F32), pltpu.VMEM((2, hs, t, t), F32)],
        compiler_params=_params(3),
        name="mla_attn",
    )(q, k, vt)


def _diff_kernel(q_ref, k_ref, vt_ref, lam_ref, g_ref, o_ref, acc_ref, s_ref, *, t, heads, lam_init):
    qi = pl.program_id(2)
    lane = lax.broadcasted_iota(jnp.int32, (t, LANES), 1)
    qs = []
    for hd in range(heads):
        q = q_ref[0, :, hd * LANES:(hd + 1) * LANES]
        zero = jnp.zeros_like(q)
        qs += [jnp.where(lane < DIFF_DH, q, zero), jnp.where(lane < DIFF_DH, zero, q)]

    def k_of(j):
        off = pl.multiple_of(j * t, t)
        ks = [k_ref[0, pl.ds(off, t), hd * LANES:(hd + 1) * LANES] for hd in range(heads)]
        return [ks[c // 2] for c in range(2 * heads)]

    def vt_of(j):
        vts = [vt_ref[0, j, hd * DIFF_V:(hd + 1) * DIFF_V, :] for hd in range(heads)]
        return [vts[c // 2] for c in range(2 * heads)]

    ls = _causal_attend(qs, k_of, vt_of, qi, t, acc_ref, s_ref)
    lp = lam_ref[...]
    lam = (jnp.exp(jnp.sum(lp[0:1] * lp[1:2], axis=-1, keepdims=True))
           - jnp.exp(jnp.sum(lp[2:3] * lp[3:4], axis=-1, keepdims=True)) + lam_init)
    for hd in range(heads):
        o1 = acc_ref[2 * hd] / ls[2 * hd]
        o2 = acc_ref[2 * hd + 1] / ls[2 * hd + 1]
        od = (o1 - lam * o2).T
        o_ref[0, :, hd * DIFF_V:(hd + 1) * DIFF_V] = (
            _rms(od, g_ref[...]) * (1.0 - lam_init)).astype(o_ref.dtype)


def _diff_call(q, k, vt, lam_p, g_sub, t, lam_init):
    b, l, _ = q.shape
    nq = l // t
    hs = DIFF_HEADS_PER_STEP
    return pl.pallas_call(
        functools.partial(_diff_kernel, t=t, heads=hs, lam_init=lam_init),
        grid=(b, DIFF_HEADS // hs, nq),
        in_specs=[
            pl.BlockSpec((1, t, hs * LANES), lambda bi, hg, qi: (bi, qi, hg)),
            pl.BlockSpec((1, l, hs * LANES), lambda bi, hg, qi: (bi, 0, hg)),
            pl.BlockSpec((1, nq, hs * DIFF_V, t), lambda bi, hg, qi: (bi, 0, hg, 0)),
            _const_spec(lam_p.shape), _const_spec(g_sub.shape),
        ],
        out_specs=pl.BlockSpec((1, t, hs * DIFF_V), lambda bi, hg, qi: (bi, qi, hg)),
        out_shape=jax.ShapeDtypeStruct((b, l, DIFF_W), BF16),
        scratch_shapes=[pltpu.VMEM((2 * hs, DIFF_V, t), F32), pltpu.VMEM((2, 2 * hs, t, t), F32)],
        compiler_params=_params(3),
        name="diff_attn",
    )(q, k, vt, lam_p, g_sub)


def _merge_kernel(r_ref, oa_ref, od_ref, g_pre_ref, wg_ref, woa_ref, wod_ref, wout_ref, g_post_ref, o_ref):
    r = r_ref[...]
    d = r.shape[-1]
    h = _rms(r, g_pre_ref[...]).astype(BF16)
    g = _dot(h, wg_ref[...])
    a = _dot(oa_ref[...], woa_ref[...])
    bb = _dot(od_ref[...], wod_ref[...])
    mixed = (jax.nn.sigmoid(g[:, :d]) * a + jax.nn.sigmoid(g[:, d:]) * bb).astype(BF16)
    mix = _dot(mixed, wout_ref[...])
    o_ref[...] = r + _rms(mix, g_post_ref[...])


def _merge_call(r2, oa2, od2, g_pre, wg, woa, wod, wout, g_post, tm):
    n, d = r2.shape
    row = lambda w: pl.BlockSpec((tm, w), lambda i: (i, 0))
    return pl.pallas_call(
        _merge_kernel,
        grid=(n // tm,),
        in_specs=[row(d), row(oa2.shape[1]), row(od2.shape[1]),
                  _const_spec(g_pre.shape), _const_spec(wg.shape), _const_spec(woa.shape),
                  _const_spec(wod.shape), _const_spec(wout.shape), _const_spec(g_post.shape)],
        out_specs=row(d),
        out_shape=jax.ShapeDtypeStruct((n, d), F32),
        compiler_params=_params(1),
        name="merge",
    )(r2, oa2, od2, g_pre, wg, woa, wod, wout, g_post)


def _ffn_kernel(r_ref, g_pre_ref, wgu_ref, wd_ref, g_post_ref, o_ref):
    r = r_ref[...]
    dff = wd_ref.shape[0]
    h = _rms(r, g_pre_ref[...]).astype(BF16)
    gu = _dot(h, wgu_ref[...])
    gate = gu[:, :dff]
    act = (gate * jax.nn.sigmoid(gate) * gu[:, dff:]).astype(BF16)
    ff = _dot(act, wd_ref[...])
    o_ref[...] = r + _rms(ff, g_post_ref[...])


def _ffn_call(r2, g_pre, wgu, wd, g_post, tm):
    n, d = r2.shape
    row = pl.BlockSpec((tm, d), lambda i: (i, 0))
    return pl.pallas_call(
        _ffn_kernel,
        grid=(n // tm,),
        in_specs=[row, _const_spec(g_pre.shape), _const_spec(wgu.shape),
                  _const_spec(wd.shape), _const_spec(g_post.shape)],
        out_specs=row,
        out_shape=jax.ShapeDtypeStruct((n, d), F32),
        compiler_params=_params(1),
        name="ffn",
    )(r2, g_pre, wgu, wd, g_post)


def _rope_tables(pos, rot_dim, group_starts, scale):
    half = rot_dim // 2
    lane = jnp.arange(LANES)
    first = jnp.zeros((LANES,), bool)
    second = jnp.zeros((LANES,), bool)
    for g in group_starts:
        first = first | ((lane >= g) & (lane < g + half))
        second = second | ((lane >= g + half) & (lane < g + rot_dim))
    freq = jnp.where(first | second, (lane % half).astype(F32), 0.0)
    inv = ROPE_THETA ** (-(2.0 * freq) / rot_dim)
    ang = pos.astype(F32)[:, None] * inv[None, :]
    cos, sin = jnp.cos(ang), jnp.sin(ang)
    c = jnp.where(first | second, cos, 1.0)
    sa = jnp.where(first, -sin, 0.0)
    sb = jnp.where(second, sin, 0.0)
    return jnp.stack([c, sa, sb]) * scale


def _pick_tile(n, candidates):
    for c in candidates:
        if n % c == 0:
            return c
    raise ValueError(f"no tile in {candidates} divides {n}")


def kernel(x, meta_tokens, norm_mix_pre, norm_mix_post, norm_ffn_pre, norm_ffn_post, w_in, mla_q_norm, mla_kv_norm, mla_w_uq, mla_w_ukv, w_o_mla, diff_lambda, diff_subln, w_o_diff, w_out, w_gate_up, w_down):
    dt = x.dtype
    b, s, d = x.shape
    depth = w_in.shape[0]
    length = s + N_META
    l_pad = -(-length // Q_BLOCK) * Q_BLOCK
    meta = jnp.broadcast_to(meta_tokens.astype(dt)[None], (b, N_META, d))
    r = jnp.concatenate([meta, x, jnp.zeros((b, l_pad - length, d), dt)], axis=1)

    t_seq = _pick_tile(l_pad, (384, 128))
    t_tok = _pick_tile(b * l_pad, (512, 384, 256, 128))

    mla_scale = LOG2E / math.sqrt(MLA_NOPE + MLA_ROPE)
    diff_scale = LOG2E / math.sqrt(DIFF_DH)
    pos = jnp.arange(l_pad)
    tab_q = _rope_tables(pos, MLA_ROPE, (MLA_NOPE,), mla_scale)
    tab_k = _rope_tables(pos, MLA_ROPE, (MLA_NOPE,), 1.0)
    tab_dq = _rope_tables(pos, DIFF_ROPE, (0, DIFF_DH), diff_scale)
    tab_dk = _rope_tables(pos, DIFF_ROPE, (0, DIFF_DH), 1.0)

    c0 = MLA_Q_RANK
    c1 = c0 + MLA_KV_RANK
    c2 = c1 + MLA_ROPE
    c_dv = c2 + 2 * DIFF_W
    c3 = c_dv + DIFF_W
    row1 = lambda v: v.reshape(1, -1).astype(F32)

    for layer in range(depth):
        wl = w_in[layer]
        kpe_w = jnp.zeros((d, LANES), dt).at[:, MLA_NOPE:MLA_NOPE + MLA_ROPE].set(wl[:, c1:c2])
        w1 = jnp.concatenate([wl[:, :c1], kpe_w, wl[:, c2:c_dv]], axis=1).astype(BF16)
        wdvt = wl[:, c_dv:c3].T.astype(BF16)
        wg = wl[:, c3:].astype(BF16)
        wuq = jnp.pad(mla_w_uq[layer].reshape(MLA_Q_RANK, MLA_HEADS, MLA_NOPE + MLA_ROPE),
                      ((0, 0), (0, 0), (0, LANES - MLA_NOPE - MLA_ROPE))
                      ).reshape(MLA_Q_RANK, MLA_W).astype(BF16)
        wukv = mla_w_ukv[layer].reshape(MLA_KV_RANK, MLA_HEADS, MLA_NOPE + MLA_V)
        wk = jnp.pad(wukv[:, :, :MLA_NOPE], ((0, 0), (0, 0), (0, LANES - MLA_NOPE))
                     ).reshape(MLA_KV_RANK, MLA_W).astype(BF16)
        wvt = wukv[:, :, MLA_NOPE:].reshape(MLA_KV_RANK, MLA_HEADS * MLA_V).T.astype(BF16)

        q, k, vt, dq, dk, dvt = _proj_call(
            r, row1(norm_mix_pre[layer]), w1, wdvt, row1(mla_q_norm[layer]), wuq,
            row1(mla_kv_norm[layer]), wk, wvt, tab_q, tab_k, tab_dq, tab_dk, t_seq)

        out_a = _mla_call(q, k, vt, t_seq)
        lam_init = 0.8 - 0.6 * math.exp(-0.3 * layer)
        out_d = _diff_call(dq, dk, dvt, diff_lambda[layer].astype(F32), row1(diff_subln[layer]),
                           t_seq, lam_init)

        r2 = r.reshape(b * l_pad, d)
        r2 = _merge_call(r2, out_a.reshape(b * l_pad, -1), out_d.reshape(b * l_pad, -1),
                         row1(norm_mix_pre[layer]), wg, w_o_mla[layer].astype(BF16),
                         w_o_diff[layer].astype(BF16), w_out[layer].astype(BF16),
                         row1(norm_mix_post[layer]), t_tok)
        r2 = _ffn_call(r2, row1(norm_ffn_pre[layer]), w_gate_up[layer].astype(BF16),
                       w_down[layer].astype(BF16), row1(norm_ffn_post[layer]), t_tok)
        r = r2.reshape(b, l_pad, d)

    return r[:, N_META:N_META + s, :]
```

```python
import functools
import math

import jax
import jax.numpy as jnp
from jax import lax
from jax.experimental import pallas as pl
from jax.experimental.pallas import tpu as pltpu

N_META = 16
Q_BLOCK = 128
ROPE_THETA = 500000.0
RMS_EPS = 1e-6

MLA_HEADS = 8
MLA_NOPE = 64
MLA_ROPE = 32
MLA_V = 64
MLA_Q_RANK = 384
MLA_KV_RANK = 256

DIFF_HEADS = 4
DIFF_DH = 64
DIFF_ROPE = DIFF_DH // 4
DIFF_V = 2 * DIFF_DH

LANES = 128
MLA_W = MLA_HEADS * LANES
DIFF_W = DIFF_HEADS * DIFF_V
MLA_HEADS_PER_STEP = 4
DIFF_HEADS_PER_STEP = 2
VMEM_LIMIT_BYTES = 56 * 1024 * 1024
MASK_VALUE = -1e30
LOG2E = 1.4426950408889634

BF16 = jnp.bfloat16
F32 = jnp.float32


def _rms(x, gain):
    ms = jnp.mean(x * x, axis=-1, keepdims=True)
    return x * lax.rsqrt(ms + RMS_EPS) * gain


def _rope(x, c, sa, sb, half):
    return x * c + pltpu.roll(x, LANES - half, 1) * sa + pltpu.roll(x, half, 1) * sb


def _dot(a, b):
    return jnp.dot(a, b, preferred_element_type=F32)


def _dot_nt(a, b):
    return lax.dot_general(a, b, (((1,), (1,)), ((), ())), preferred_element_type=F32)


def _const_spec(shape):
    nd = len(shape)
    return pl.BlockSpec(shape, lambda *_: (0,) * nd, pipeline_mode=pl.Buffered(1))


def _params(n_grid):
    return pltpu.CompilerParams(
        dimension_semantics=("arbitrary",) * n_grid,
        vmem_limit_bytes=VMEM_LIMIT_BYTES,
    )


def _proj_kernel(r_ref, g_pre_ref, w1_ref, wdvt_ref, g_q_ref, wuq_ref, g_kv_ref, wk_ref, wvt_ref,
                 tq_ref, tk_ref, td_ref, tdk_ref,
                 q_ref, k_ref, vt_ref, dq_ref, dk_ref, dvt_ref):
    r = r_ref[0]
    h = _rms(r, g_pre_ref[...]).astype(BF16)
    p1 = _dot(h, w1_ref[...])
    o = 0
    q_lat = p1[:, o:o + MLA_Q_RANK]; o += MLA_Q_RANK
    kv_lat = p1[:, o:o + MLA_KV_RANK]; o += MLA_KV_RANK
    kpe = p1[:, o:o + LANES]; o += LANES
    dq = p1[:, o:o + DIFF_W]; o += DIFF_W
    dk = p1[:, o:o + DIFF_W]

    cq = _rms(q_lat, g_q_ref[...]).astype(BF16)
    qa = _dot(cq, wuq_ref[...])
    cqt, saq, sbq = tq_ref[0], tq_ref[1], tq_ref[2]
    for hd in range(MLA_HEADS):
        sl = slice(hd * LANES, (hd + 1) * LANES)
        q_ref[0, :, sl] = _rope(qa[:, sl], cqt, saq, sbq, MLA_ROPE // 2).astype(BF16)

    ckv = _rms(kv_lat, g_kv_ref[...]).astype(BF16)
    kn = _dot(ckv, wk_ref[...])
    kpe_r = _rope(kpe, tk_ref[0], tk_ref[1], tk_ref[2], MLA_ROPE // 2)
    for hd in range(MLA_HEADS):
        sl = slice(hd * LANES, (hd + 1) * LANES)
        k_ref[0, :, sl] = (kn[:, sl] + kpe_r).astype(BF16)
    vt_ref[0, 0] = _dot_nt(wvt_ref[...], ckv).astype(BF16)

    for hd in range(DIFF_HEADS):
        sl = slice(hd * LANES, (hd + 1) * LANES)
        dq_ref[0, :, sl] = _rope(dq[:, sl], td_ref[0], td_ref[1], td_ref[2], DIFF_ROPE // 2).astype(BF16)
        dk_ref[0, :, sl] = _rope(dk[:, sl], tdk_ref[0], tdk_ref[1], tdk_ref[2], DIFF_ROPE // 2).astype(BF16)
    dvt_ref[0, 0] = _dot_nt(wdvt_ref[...], h).astype(BF16)


def _proj_call(r, g_pre, w1, wdvt, g_q, wuq, g_kv, wk, wvt, tq, tk, td, tdk, tm):
    b, l, d = r.shape
    nt = l // tm
    tab_spec = pl.BlockSpec((3, tm, LANES), lambda bi, ti: (0, ti, 0))

    def rows(n):
        return pl.BlockSpec((1, tm, n), lambda bi, ti: (bi, ti, 0))

    def cols(n):
        return pl.BlockSpec((1, 1, n, tm), lambda bi, ti: (bi, ti, 0, 0))

    nv, ndv = wvt.shape[0], wdvt.shape[0]
    return pl.pallas_call(
        _proj_kernel,
        grid=(b, nt),
        in_specs=[
            pl.BlockSpec((1, tm, d), lambda bi, ti: (bi, ti, 0)),
            _const_spec(g_pre.shape), _const_spec(w1.shape), _const_spec(wdvt.shape),
            _const_spec(g_q.shape), _const_spec(wuq.shape),
            _const_spec(g_kv.shape), _const_spec(wk.shape), _const_spec(wvt.shape),
            tab_spec, tab_spec, tab_spec, tab_spec,
        ],
        out_specs=[rows(MLA_W), rows(MLA_W), cols(nv), rows(DIFF_W), rows(DIFF_W), cols(ndv)],
        out_shape=[
            jax.ShapeDtypeStruct((b, l, MLA_W), BF16),
            jax.ShapeDtypeStruct((b, l, MLA_W), BF16),
            jax.ShapeDtypeStruct((b, nt, nv, tm), BF16),
            jax.ShapeDtypeStruct((b, l, DIFF_W), BF16),
            jax.ShapeDtypeStruct((b, l, DIFF_W), BF16),
            jax.ShapeDtypeStruct((b, nt, ndv, tm), BF16),
        ],
        compiler_params=_params(2),
        name="proj",
    )(r, g_pre, w1, wdvt, g_q, wuq, g_kv, wk, wvt, tq, tk, td, tdk)


def _attend_query_tiles(q_of, k_of, vt_of, write_out, nq, t, n, acc_ref, s_ref):
    m0 = jnp.full((1, t), MASK_VALUE, F32)
    l0 = jnp.zeros((1, t), F32)
    key = lax.broadcasted_iota(jnp.int32, (t, t), 0)
    qry = lax.broadcasted_iota(jnp.int32, (t, t), 1)
    diag = key <= qry

    def matmul_scores(qi, j):
        qs, ks = q_of(qi), k_of(j)
        return [_dot_nt(ks[c], qs[c]) for c in range(n)]

    def park_scores(ss, slot, mask):
        tile_max = []
        for c in range(n):
            s = ss[c] if mask is None else jnp.where(mask, ss[c], MASK_VALUE)
            s_ref[slot, c] = s
            tile_max.append(jnp.max(s, axis=0, keepdims=True))
        return tuple(tile_max)

    def update(j, slot, tile_max, ms, ls):
        vts = vt_of(j)
        ms_new, ls_new = [], []
        for c in range(n):
            m_new = jnp.maximum(ms[c], tile_max[c])
            alpha = jnp.exp2(ms[c] - m_new)
            p = jnp.exp2(s_ref[slot, c] - m_new)
            ls_new.append(alpha * ls[c] + jnp.sum(p, axis=0, keepdims=True))
            acc_ref[c] = alpha * acc_ref[c] + _dot(vts[c], p.astype(BF16))
            ms_new.append(m_new)
        return tuple(ms_new), tuple(ls_new)

    def query_tile(qi, first_max):
        acc_ref[...] = jnp.zeros_like(acc_ref)

        def step(j, slot, carry, mask):
            tile_max, ms, ls = carry
            nxt = park_scores(matmul_scores(qi, j + 1), 1 - slot, mask)
            ms, ls = update(j, slot, tile_max, ms, ls)
            return nxt, ms, ls

        def pair(i, carry):
            carry = step(2 * i, 0, carry, None)
            return step(2 * i + 1, 1, carry, None)

        def last(slot, carry):
            ss = matmul_scores(jnp.minimum(qi + 1, nq - 1), 0)
            _, ls = update(qi, slot, *carry)
            return ls, park_scores(ss, 0, None)

        def finish_even(carry):
            return last(1, step(qi - 1, 0, carry, diag))

        def finish_odd(carry):
            carry = step(qi - 2, 0, carry, None)
            return last(0, step(qi - 1, 1, carry, diag))

        def finish(carry):
            return lax.cond(((qi - 1) & 1) == 0, finish_even, finish_odd, carry)

        def only_diagonal(carry):
            return last(0, carry)

        carry = (first_max, (m0,) * n, (l0,) * n)
        carry = lax.fori_loop(0, jnp.maximum(qi - 1, 0) // 2, pair, carry)
        ls, next_first = lax.cond(qi >= 1, finish, only_diagonal, carry)
        write_out(qi, ls)
        return next_first

    lax.fori_loop(0, nq, query_tile, park_scores(matmul_scores(0, 0), 0, diag))


def _mla_kernel(q_ref, k_ref, vt_ref, o_ref, acc_ref, s_ref, *, t, nq, heads):
    lane = [slice(c * LANES, (c + 1) * LANES) for c in range(heads)]
    rows = [slice(c * MLA_V, (c + 1) * MLA_V) for c in range(heads)]

    def q_of(qi):
        off = pl.multiple_of(qi * t, t)
        return [q_ref[0, pl.ds(off, t), lane[c]] for c in range(heads)]

    def k_of(j):
        off = pl.multiple_of(j * t, t)
        return [k_ref[0, pl.ds(off, t), lane[c]] for c in range(heads)]

    def vt_of(j):
        return [vt_ref[0, j, rows[c], :] for c in range(heads)]

    def write_out(qi, ls):
        out_t = jnp.concatenate([acc_ref[c] / ls[c] for c in range(heads)], axis=0)
        o_ref[0, pl.ds(pl.multiple_of(qi * t, t), t), :] = out_t.T.astype(o_ref.dtype)

    _attend_query_tiles(q_of, k_of, vt_of, write_out, nq, t, heads, acc_ref, s_ref)


def _mla_call(q, k, vt, t):
    b, l, _ = q.shape
    nq = l // t
    hs = MLA_HEADS_PER_STEP
    return pl.pallas_call(
        functools.partial(_mla_kernel, t=t, nq=nq, heads=hs),
        grid=(b, MLA_HEADS // hs),
        in_specs=[
            pl.BlockSpec((1, l, hs * LANES), lambda bi, hg: (bi, 0, hg)),
            pl.BlockSpec((1, l, hs * LANES), lambda bi, hg: (bi, 0, hg)),
            pl.BlockSpec((1, nq, hs * MLA_V, t), lambda bi, hg: (bi, 0, hg, 0)),
        ],
        out_specs=pl.BlockSpec((1, l, hs * MLA_V), lambda bi, hg: (bi, 0, hg)),
        out_shape=jax.ShapeDtypeStruct((b, l, MLA_HEADS * MLA_V), BF16),
        scratch_shapes=[pltpu.VMEM((hs, MLA_V, t), F32), pltpu.VMEM((2, hs, t, t), F32)],
        compiler_params=_params(2),
        name="mla_attn",
    )(q, k, vt)


def _diff_kernel(q_ref, k_ref, vt_ref, lam_ref, g_ref, o_ref, acc_ref, s_ref, *, t, nq, heads, lam_init):
    lane = lax.broadcasted_iota(jnp.int32, (t, LANES), 1)

    def q_of(qi):
        off = pl.multiple_of(qi * t, t)
        qs = []
        for hd in range(heads):
            q = q_ref[0, pl.ds(off, t), hd * LANES:(hd + 1) * LANES]
            zero = jnp.zeros_like(q)
            qs += [jnp.where(lane < DIFF_DH, q, zero), jnp.where(lane < DIFF_DH, zero, q)]
        return qs

    def k_of(j):
        off = pl.multiple_of(j * t, t)
        ks = [k_ref[0, pl.ds(off, t), hd * LANES:(hd + 1) * LANES] for hd in range(heads)]
        return [ks[c // 2] for c in range(2 * heads)]

    def vt_of(j):
        vts = [vt_ref[0, j, hd * DIFF_V:(hd + 1) * DIFF_V, :] for hd in range(heads)]
        return [vts[c // 2] for c in range(2 * heads)]

    lp = lam_ref[...]
    lam = (jnp.exp(jnp.sum(lp[0:1] * lp[1:2], axis=-1, keepdims=True))
           - jnp.exp(jnp.sum(lp[2:3] * lp[3:4], axis=-1, keepdims=True)) + lam_init)

    def write_out(qi, ls):
        off = pl.multiple_of(qi * t, t)
        for hd in range(heads):
            o1 = acc_ref[2 * hd] / ls[2 * hd]
            o2 = acc_ref[2 * hd + 1] / ls[2 * hd + 1]
            od = (o1 - lam * o2).T
            o_ref[0, pl.ds(off, t), hd * DIFF_V:(hd + 1) * DIFF_V] = (
                _rms(od, g_ref[...]) * (1.0 - lam_init)).astype(o_ref.dtype)

    _attend_query_tiles(q_of, k_of, vt_of, write_out, nq, t, 2 * heads, acc_ref, s_ref)


def _diff_call(q, k, vt, lam_p, g_sub, t, lam_init):
    b, l, _ = q.shape
    nq = l // t
    hs = DIFF_HEADS_PER_STEP
    return pl.pallas_call(
        functools.partial(_diff_kernel, t=t, nq=nq, heads=hs, lam_init=lam_init),
        grid=(b, DIFF_HEADS // hs),
        in_specs=[
            pl.BlockSpec((1, l, hs * LANES), lambda bi, hg: (bi, 0, hg)),
            pl.BlockSpec((1, l, hs * LANES), lambda bi, hg: (bi, 0, hg)),
            pl.BlockSpec((1, nq, hs * DIFF_V, t), lambda bi, hg: (bi, 0, hg, 0)),
            _const_spec(lam_p.shape), _const_spec(g_sub.shape),
        ],
        out_specs=pl.BlockSpec((1, l, hs * DIFF_V), lambda bi, hg: (bi, 0, hg)),
        out_shape=jax.ShapeDtypeStruct((b, l, DIFF_W), BF16),
        scratch_shapes=[pltpu.VMEM((2 * hs, DIFF_V, t), F32), pltpu.VMEM((2, 2 * hs, t, t), F32)],
        compiler_params=_params(2),
        name="diff_attn",
    )(q, k, vt, lam_p, g_sub)


def _merge_kernel(r_ref, oa_ref, od_ref, g_pre_ref, wg_ref, woa_ref, wod_ref, wout_ref, g_post_ref, o_ref):
    r = r_ref[...]
    d = r.shape[-1]
    h = _rms(r, g_pre_ref[...]).astype(BF16)
    g = _dot(h, wg_ref[...])
    a = _dot(oa_ref[...], woa_ref[...])
    bb = _dot(od_ref[...], wod_ref[...])
    mixed = (jax.nn.sigmoid(g[:, :d]) * a + jax.nn.sigmoid(g[:, d:]) * bb).astype(BF16)
    mix = _dot(mixed, wout_ref[...])
    o_ref[...] = r + _rms(mix, g_post_ref[...])


def _merge_call(r2, oa2, od2, g_pre, wg, woa, wod, wout, g_post, tm):
    n, d = r2.shape
    row = lambda w: pl.BlockSpec((tm, w), lambda i: (i, 0))
    return pl.pallas_call(
        _merge_kernel,
        grid=(n // tm,),
        in_specs=[row(d), row(oa2.shape[1]), row(od2.shape[1]),
                  _const_spec(g_pre.shape), _const_spec(wg.shape), _const_spec(woa.shape),
                  _const_spec(wod.shape), _const_spec(wout.shape), _const_spec(g_post.shape)],
        out_specs=row(d),
        out_shape=jax.ShapeDtypeStruct((n, d), F32),
        compiler_params=_params(1),
        name="merge",
    )(r2, oa2, od2, g_pre, wg, woa, wod, wout, g_post)


def _ffn_kernel(r_ref, g_pre_ref, wgu_ref, wd_ref, g_post_ref, o_ref):
    r = r_ref[...]
    dff = wd_ref.shape[0]
    h = _rms(r, g_pre_ref[...]).astype(BF16)
    gu = _dot(h, wgu_ref[...])
    gate = gu[:, :dff]
    act = (gate * jax.nn.sigmoid(gate) * gu[:, dff:]).astype(BF16)
    ff = _dot(act, wd_ref[...])
    o_ref[...] = r + _rms(ff, g_post_ref[...])


def _ffn_call(r2, g_pre, wgu, wd, g_post, tm):
    n, d = r2.shape
    row = pl.BlockSpec((tm, d), lambda i: (i, 0))
    return pl.pallas_call(
        _ffn_kernel,
        grid=(n // tm,),
        in_specs=[row, _const_spec(g_pre.shape), _const_spec(wgu.shape),
                  _const_spec(wd.shape), _const_spec(g_post.shape)],
        out_specs=row,
        out_shape=jax.ShapeDtypeStruct((n, d), F32),
        compiler_params=_params(1),
        name="ffn",
    )(r2, g_pre, wgu, wd, g_post)


def _ffn_last_kernel(r_ref, g_pre_ref, wgu_ref, wd_ref, g_post_ref, o_ref):
    _ffn_kernel(r_ref.at[0], g_pre_ref, wgu_ref, wd_ref, g_post_ref, o_ref.at[0])


def _ffn_last_call(r, g_pre, wgu, wd, g_post, s, tm):
    b, _, d = r.shape
    return pl.pallas_call(
        _ffn_last_kernel,
        grid=(b, s // tm),
        in_specs=[pl.BlockSpec((pl.Element(1), pl.Element(tm), pl.Element(d)),
                               lambda bi, i: (bi, pl.multiple_of(N_META + i * tm, N_META), 0)),
                  _const_spec(g_pre.shape), _const_spec(wgu.shape),
                  _const_spec(wd.shape), _const_spec(g_post.shape)],
        out_specs=pl.BlockSpec((1, tm, d), lambda bi, i: (bi, i, 0)),
        out_shape=jax.ShapeDtypeStruct((b, s, d), F32),
        compiler_params=_params(2),
        name="ffn_last",
    )(r, g_pre, wgu, wd, g_post)


def _rope_tables(pos, rot_dim, group_starts, scale):
    half = rot_dim // 2
    lane = jnp.arange(LANES)
    first = jnp.zeros((LANES,), bool)
    second = jnp.zeros((LANES,), bool)
    for g in group_starts:
        first = first | ((lane >= g) & (lane < g + half))
        second = second | ((lane >= g + half) & (lane < g + rot_dim))
    freq = jnp.where(first | second, (lane % half).astype(F32), 0.0)
    inv = ROPE_THETA ** (-(2.0 * freq) / rot_dim)
    ang = pos.astype(F32)[:, None] * inv[None, :]
    cos, sin = jnp.cos(ang), jnp.sin(ang)
    c = jnp.where(first | second, cos, 1.0)
    sa = jnp.where(first, -sin, 0.0)
    sb = jnp.where(second, sin, 0.0)
    return jnp.stack([c, sa, sb]) * scale


def _pick_tile(n, candidates):
    for c in candidates:
        if n % c == 0:
            return c
    raise ValueError(f"no tile in {candidates} divides {n}")


def kernel(x, meta_tokens, norm_mix_pre, norm_mix_post, norm_ffn_pre, norm_ffn_post, w_in, mla_q_norm, mla_kv_norm, mla_w_uq, mla_w_ukv, w_o_mla, diff_lambda, diff_subln, w_o_diff, w_out, w_gate_up, w_down):
    dt = x.dtype
    b, s, d = x.shape
    depth = w_in.shape[0]
    length = s + N_META
    l_pad = -(-length // Q_BLOCK) * Q_BLOCK
    meta = jnp.broadcast_to(meta_tokens.astype(dt)[None], (b, N_META, d))
    r = jnp.concatenate([meta, x, jnp.zeros((b, l_pad - length, d), dt)], axis=1)

    t_seq = _pick_tile(l_pad, (384, 128))
    t_tok = _pick_tile(b * l_pad, (512, 384, 256, 128))

    mla_scale = LOG2E / math.sqrt(MLA_NOPE + MLA_ROPE)
    diff_scale = LOG2E / math.sqrt(DIFF_DH)
    pos = jnp.arange(l_pad)
    tab_q = _rope_tables(pos, MLA_ROPE, (MLA_NOPE,), mla_scale)
    tab_k = _rope_tables(pos, MLA_ROPE, (MLA_NOPE,), 1.0)
    tab_dq = _rope_tables(pos, DIFF_ROPE, (0, DIFF_DH), diff_scale)
    tab_dk = _rope_tables(pos, DIFF_ROPE, (0, DIFF_DH), 1.0)

    c0 = MLA_Q_RANK
    c1 = c0 + MLA_KV_RANK
    c2 = c1 + MLA_ROPE
    c_dv = c2 + 2 * DIFF_W
    c3 = c_dv + DIFF_W
    row1 = lambda v: v.reshape(1, -1).astype(F32)

    for layer in range(depth):
        wl = w_in[layer]
        kpe_w = jnp.zeros((d, LANES), dt).at[:, MLA_NOPE:MLA_NOPE + MLA_ROPE].set(wl[:, c1:c2])
        w1 = jnp.concatenate([wl[:, :c1], kpe_w, wl[:, c2:c_dv]], axis=1).astype(BF16)
        wdvt = wl[:, c_dv:c3].T.astype(BF16)
        wg = wl[:, c3:].astype(BF16)
        wuq = jnp.pad(mla_w_uq[layer].reshape(MLA_Q_RANK, MLA_HEADS, MLA_NOPE + MLA_ROPE),
                      ((0, 0), (0, 0), (0, LANES - MLA_NOPE - MLA_ROPE))
                      ).reshape(MLA_Q_RANK, MLA_W).astype(BF16)
        wukv = mla_w_ukv[layer].reshape(MLA_KV_RANK, MLA_HEADS, MLA_NOPE + MLA_V)
        wk = jnp.pad(wukv[:, :, :MLA_NOPE], ((0, 0), (0, 0), (0, LANES - MLA_NOPE))
                     ).reshape(MLA_KV_RANK, MLA_W).astype(BF16)
        wvt = wukv[:, :, MLA_NOPE:].reshape(MLA_KV_RANK, MLA_HEADS * MLA_V).T.astype(BF16)

        q, k, vt, dq, dk, dvt = _proj_call(
            r, row1(norm_mix_pre[layer]), w1, wdvt, row1(mla_q_norm[layer]), wuq,
            row1(mla_kv_norm[layer]), wk, wvt, tab_q, tab_k, tab_dq, tab_dk, t_seq)

        out_a = _mla_call(q, k, vt, t_seq)
        lam_init = 0.8 - 0.6 * math.exp(-0.3 * layer)
        out_d = _diff_call(dq, dk, dvt, diff_lambda[layer].astype(F32), row1(diff_subln[layer]),
                           t_seq, lam_init)

        r2 = r.reshape(b * l_pad, d)
        r2 = _merge_call(r2, out_a.reshape(b * l_pad, -1), out_d.reshape(b * l_pad, -1),
                         row1(norm_mix_pre[layer]), wg, w_o_mla[layer].astype(BF16),
                         w_o_diff[layer].astype(BF16), w_out[layer].astype(BF16),
                         row1(norm_mix_post[layer]), t_tok)
        ffn_args = (row1(norm_ffn_pre[layer]), w_gate_up[layer].astype(BF16),
                    w_down[layer].astype(BF16), row1(norm_ffn_post[layer]))
        if layer == depth - 1:
            return _ffn_last_call(r2.reshape(b, l_pad, d), *ffn_args, s,
                                  _pick_tile(s, (512, 256, 128, N_META)))
        r = _ffn_call(r2, *ffn_args, t_tok).reshape(b, l_pad, d)
```

```python
import functools
import math

import jax
import jax.numpy as jnp
from jax import lax
from jax.experimental import pallas as pl
from jax.experimental.pallas import tpu as pltpu

N_META = 16
META_BLOCK = 128
META_KEYS = 16
ROPE_THETA = 500000.0
RMS_EPS = 1e-6

MLA_HEADS = 8
MLA_NOPE = 64
MLA_ROPE = 32
MLA_V = 64
MLA_Q_RANK = 384
MLA_KV_RANK = 256

DIFF_HEADS = 4
DIFF_DH = 64
DIFF_ROPE = DIFF_DH // 4
DIFF_V = 2 * DIFF_DH

LANES = 128
MLA_W = MLA_HEADS * LANES
DIFF_W = DIFF_HEADS * DIFF_V
MLA_HEADS_PER_STEP = 4
DIFF_HEADS_PER_STEP = 2
VMEM_LIMIT_BYTES = 56 * 1024 * 1024
MASK_VALUE = -1e30
LOG2E = 1.4426950408889634

BF16 = jnp.bfloat16
F32 = jnp.float32


def _rms(x, gain):
    ms = jnp.mean(x * x, axis=-1, keepdims=True)
    return x * lax.rsqrt(ms + RMS_EPS) * gain


def _rope(x, c, sa, sb, half):
    return x * c + pltpu.roll(x, LANES - half, 1) * sa + pltpu.roll(x, half, 1) * sb


def _dot(a, b):
    return jnp.dot(a, b, preferred_element_type=F32)


def _dot_nt(a, b):
    return lax.dot_general(a, b, (((1,), (1,)), ((), ())), preferred_element_type=F32)


def _const_spec(shape):
    nd = len(shape)
    return pl.BlockSpec(shape, lambda *_: (0,) * nd, pipeline_mode=pl.Buffered(1))


def _params(n_grid):
    return pltpu.CompilerParams(
        dimension_semantics=("arbitrary",) * n_grid,
        vmem_limit_bytes=VMEM_LIMIT_BYTES,
    )


def _proj_kernel(r_ref, g_pre_ref, w1_ref, wdvt_ref, g_q_ref, wuq_ref, g_kv_ref, wk_ref, wvt_ref,
                 tq_ref, tk_ref, td_ref, tdk_ref,
                 q_ref, k_ref, vt_ref, dq_ref, dk_ref, dvt_ref):
    r = r_ref[0]
    h = _rms(r, g_pre_ref[...]).astype(BF16)
    p1 = _dot(h, w1_ref[...])
    o = 0
    q_lat = p1[:, o:o + MLA_Q_RANK]; o += MLA_Q_RANK
    kv_lat = p1[:, o:o + MLA_KV_RANK]; o += MLA_KV_RANK
    kpe = p1[:, o:o + LANES]; o += LANES
    dq = p1[:, o:o + DIFF_W]; o += DIFF_W
    dk = p1[:, o:o + DIFF_W]

    cq = _rms(q_lat, g_q_ref[...]).astype(BF16)
    qa = _dot(cq, wuq_ref[...])
    cqt, saq, sbq = tq_ref[0], tq_ref[1], tq_ref[2]
    for hd in range(MLA_HEADS):
        sl = slice(hd * LANES, (hd + 1) * LANES)
        q_ref[0, :, sl] = _rope(qa[:, sl], cqt, saq, sbq, MLA_ROPE // 2).astype(BF16)

    ckv = _rms(kv_lat, g_kv_ref[...]).astype(BF16)
    kn = _dot(ckv, wk_ref[...])
    kpe_r = _rope(kpe, tk_ref[0], tk_ref[1], tk_ref[2], MLA_ROPE // 2)
    for hd in range(MLA_HEADS):
        sl = slice(hd * LANES, (hd + 1) * LANES)
        k_ref[0, :, sl] = (kn[:, sl] + kpe_r).astype(BF16)
    vt_ref[0, 0] = _dot_nt(wvt_ref[...], ckv).astype(BF16)

    for hd in range(DIFF_HEADS):
        sl = slice(hd * LANES, (hd + 1) * LANES)
        dq_ref[0, :, sl] = _rope(dq[:, sl], td_ref[0], td_ref[1], td_ref[2], DIFF_ROPE // 2).astype(BF16)
        dk_ref[0, :, sl] = _rope(dk[:, sl], tdk_ref[0], tdk_ref[1], tdk_ref[2], DIFF_ROPE // 2).astype(BF16)
    dvt_ref[0, 0] = _dot_nt(wdvt_ref[...], h).astype(BF16)


def _proj_call(r, g_pre, w1, wdvt, g_q, wuq, g_kv, wk, wvt, tq, tk, td, tdk, tm):
    b, l, d = r.shape
    nt = l // tm
    tab_spec = pl.BlockSpec((3, tm, LANES), lambda bi, ti: (0, ti, 0))

    def rows(n):
        return pl.BlockSpec((1, tm, n), lambda bi, ti: (bi, ti, 0))

    def cols(n):
        return pl.BlockSpec((1, 1, n, tm), lambda bi, ti: (bi, ti, 0, 0))

    nv, ndv = wvt.shape[0], wdvt.shape[0]
    return pl.pallas_call(
        _proj_kernel,
        grid=(b, nt),
        in_specs=[
            pl.BlockSpec((1, tm, d), lambda bi, ti: (bi, ti, 0)),
            _const_spec(g_pre.shape), _const_spec(w1.shape), _const_spec(wdvt.shape),
            _const_spec(g_q.shape), _const_spec(wuq.shape),
            _const_spec(g_kv.shape), _const_spec(wk.shape), _const_spec(wvt.shape),
            tab_spec, tab_spec, tab_spec, tab_spec,
        ],
        out_specs=[rows(MLA_W), rows(MLA_W), cols(nv), rows(DIFF_W), rows(DIFF_W), cols(ndv)],
        out_shape=[
            jax.ShapeDtypeStruct((b, l, MLA_W), BF16),
            jax.ShapeDtypeStruct((b, l, MLA_W), BF16),
            jax.ShapeDtypeStruct((b, nt, nv, tm), BF16),
            jax.ShapeDtypeStruct((b, l, DIFF_W), BF16),
            jax.ShapeDtypeStruct((b, l, DIFF_W), BF16),
            jax.ShapeDtypeStruct((b, nt, ndv, tm), BF16),
        ],
        compiler_params=_params(2),
        name="proj",
    )(r, g_pre, w1, wdvt, g_q, wuq, g_kv, wk, wvt, tq, tk, td, tdk)


def _attend_meta(qs, meta_k, meta_vt, causal):
    tm, tq = meta_k[0].shape[0], qs[0].shape[0]
    valid = lax.broadcasted_iota(jnp.int32, (META_KEYS, tq), 0) < N_META
    if causal is not None:
        valid = jnp.logical_and(valid, causal)
    ss = [_dot_nt(meta_k[c][:META_KEYS], qs[c]) for c in range(len(qs))]
    ms, ls, ps = [], [], []
    for s in ss:
        if causal is not None or META_KEYS != N_META:
            s = jnp.where(valid, s, MASK_VALUE)
        m = jnp.max(s, axis=0, keepdims=True)
        p = jnp.exp2(s - m)
        ms.append(m)
        ls.append(jnp.sum(p, axis=0, keepdims=True))
        ps.append(jnp.concatenate([p.astype(BF16), jnp.zeros((tm - META_KEYS, tq), BF16)], axis=0))
    accs = [_dot(meta_vt[c], ps[c]) for c in range(len(qs))]
    return tuple(ms), tuple(ls), accs


def _attend_query_tiles(q_of, k_of, vt_of, meta_k, meta_vt, write_out, nq, t, n, acc_ref, accm_ref, s_ref):
    key = lax.broadcasted_iota(jnp.int32, (t, t), 0)
    qry = lax.broadcasted_iota(jnp.int32, (t, t), 1)
    diag = key <= qry

    def matmul_scores(qi, j):
        qs, ks = q_of(qi), k_of(j)
        return [_dot_nt(ks[c], qs[c]) for c in range(n)]

    def park_scores(ss, slot, mask):
        tile_max = []
        for c in range(n):
            s = ss[c] if mask is None else jnp.where(mask, ss[c], MASK_VALUE)
            s_ref[slot, c] = s
            tile_max.append(jnp.max(s, axis=0, keepdims=True))
        return tuple(tile_max)

    def update(j, slot, tile_max, ms, ls):
        vts = vt_of(j)
        ms_new, ls_new = [], []
        for c in range(n):
            m_new = jnp.maximum(ms[c], tile_max[c])
            alpha = jnp.exp2(ms[c] - m_new)
            p = jnp.exp2(s_ref[slot, c] - m_new)
            ls_new.append(alpha * ls[c] + jnp.sum(p, axis=0, keepdims=True))
            acc_ref[c] = alpha * acc_ref[c] + _dot(vts[c], p.astype(BF16))
            ms_new.append(m_new)
        return tuple(ms_new), tuple(ls_new)

    def start_query_tile(qi, mask):
        ss = matmul_scores(qi, 0)

        def rest():
            ms0, ls0, accs0 = _attend_meta(q_of(qi), meta_k, meta_vt, None)
            for c in range(n):
                accm_ref[c] = accs0[c]
            return park_scores(ss, 0, mask), ms0, ls0

        return rest

    def query_tile(qi, first):
        first_max, ms0, ls0 = first
        acc_ref[...] = accm_ref[...]

        def step(j, slot, carry, mask):
            tile_max, ms, ls = carry
            nxt = park_scores(matmul_scores(qi, j + 1), 1 - slot, mask)
            ms, ls = update(j, slot, tile_max, ms, ls)
            return nxt, ms, ls

        def pair(i, carry):
            carry = step(2 * i, 0, carry, None)
            return step(2 * i + 1, 1, carry, None)

        def last(slot, carry):
            nxt_first = start_query_tile(jnp.minimum(qi + 1, nq - 1), None)
            _, ls = update(qi, slot, *carry)
            return ls, nxt_first()

        def finish_even(carry):
            return last(1, step(qi - 1, 0, carry, diag))

        def finish_odd(carry):
            carry = step(qi - 2, 0, carry, None)
            return last(0, step(qi - 1, 1, carry, diag))

        def finish(carry):
            return lax.cond(((qi - 1) & 1) == 0, finish_even, finish_odd, carry)

        def only_diagonal(carry):
            return last(0, carry)

        carry = (first_max, ms0, ls0)
        carry = lax.fori_loop(0, jnp.maximum(qi - 1, 0) // 2, pair, carry)
        ls, next_first = lax.cond(qi >= 1, finish, only_diagonal, carry)
        write_out(qi, [acc_ref[c] for c in range(n)], ls)
        return next_first

    lax.fori_loop(0, nq, query_tile, start_query_tile(0, diag)())


def _mla_out(accs, ls):
    out_t = jnp.concatenate([accs[c] / ls[c] for c in range(len(accs))], axis=0)
    return out_t.T.astype(BF16)


def _mla_meta_operands(km_ref, vtm_ref, heads):
    meta_k = [km_ref[0, :, c * LANES:(c + 1) * LANES] for c in range(heads)]
    meta_vt = [vtm_ref[0, 0, c * MLA_V:(c + 1) * MLA_V, :] for c in range(heads)]
    return meta_k, meta_vt


def _mla_kernel(q_ref, k_ref, vt_ref, km_ref, vtm_ref, o_ref, acc_ref, accm_ref, s_ref, *, t, nq, heads):
    lane = [slice(c * LANES, (c + 1) * LANES) for c in range(heads)]
    rows = [slice(c * MLA_V, (c + 1) * MLA_V) for c in range(heads)]

    def q_of(qi):
        off = pl.multiple_of(qi * t, t)
        return [q_ref[0, pl.ds(off, t), lane[c]] for c in range(heads)]

    def k_of(j):
        off = pl.multiple_of(j * t, t)
        return [k_ref[0, pl.ds(off, t), lane[c]] for c in range(heads)]

    def vt_of(j):
        return [vt_ref[0, j, rows[c], :] for c in range(heads)]

    def write_out(qi, accs, ls):
        o_ref[0, pl.ds(pl.multiple_of(qi * t, t), t), :] = _mla_out(accs, ls)

    meta_k, meta_vt = _mla_meta_operands(km_ref, vtm_ref, heads)
    _attend_query_tiles(q_of, k_of, vt_of, meta_k, meta_vt, write_out, nq, t, heads, acc_ref, accm_ref, s_ref)


def _mla_meta_kernel(q_ref, km_ref, vtm_ref, o_ref, *, heads):
    tm = q_ref.shape[1]
    causal = (lax.broadcasted_iota(jnp.int32, (META_KEYS, tm), 0)
              <= lax.broadcasted_iota(jnp.int32, (META_KEYS, tm), 1))
    qs = [q_ref[0, :, c * LANES:(c + 1) * LANES] for c in range(heads)]
    meta_k, meta_vt = _mla_meta_operands(km_ref, vtm_ref, heads)
    _, ls, accs = _attend_meta(qs, meta_k, meta_vt, causal)
    o_ref[0] = _mla_out(accs, ls)


def _mla_call(q, k, vt, km, vtm, t):
    b, l, _ = q.shape
    nq = l // t
    hs = MLA_HEADS_PER_STEP
    tm = km.shape[1]
    return pl.pallas_call(
        functools.partial(_mla_kernel, t=t, nq=nq, heads=hs),
        grid=(b, MLA_HEADS // hs),
        in_specs=[
            pl.BlockSpec((1, l, hs * LANES), lambda bi, hg: (bi, 0, hg)),
            pl.BlockSpec((1, l, hs * LANES), lambda bi, hg: (bi, 0, hg)),
            pl.BlockSpec((1, nq, hs * MLA_V, t), lambda bi, hg: (bi, 0, hg, 0)),
            pl.BlockSpec((1, tm, hs * LANES), lambda bi, hg: (0, 0, hg)),
            pl.BlockSpec((1, 1, hs * MLA_V, tm), lambda bi, hg: (0, 0, hg, 0)),
        ],
        out_specs=pl.BlockSpec((1, l, hs * MLA_V), lambda bi, hg: (bi, 0, hg)),
        out_shape=jax.ShapeDtypeStruct((b, l, MLA_HEADS * MLA_V), BF16),
        scratch_shapes=[pltpu.VMEM((hs, MLA_V, t), F32), pltpu.VMEM((hs, MLA_V, t), F32),
                        pltpu.VMEM((2, hs, t, t), F32)],
        compiler_params=_params(2),
        name="mla_attn",
    )(q, k, vt, km, vtm)


def _mla_meta_call(qm, km, vtm):
    _, tm, _ = qm.shape
    hs = MLA_HEADS_PER_STEP
    return pl.pallas_call(
        functools.partial(_mla_meta_kernel, heads=hs),
        grid=(MLA_HEADS // hs,),
        in_specs=[
            pl.BlockSpec((1, tm, hs * LANES), lambda hg: (0, 0, hg)),
            pl.BlockSpec((1, tm, hs * LANES), lambda hg: (0, 0, hg)),
            pl.BlockSpec((1, 1, hs * MLA_V, tm), lambda hg: (0, 0, hg, 0)),
        ],
        out_specs=pl.BlockSpec((1, tm, hs * MLA_V), lambda hg: (0, 0, hg)),
        out_shape=jax.ShapeDtypeStruct((1, tm, MLA_HEADS * MLA_V), BF16),
        compiler_params=_params(1),
        name="mla_meta_attn",
    )(qm, km, vtm)


def _diff_queries(q):
    lane = lax.broadcasted_iota(jnp.int32, q.shape, 1)
    zero = jnp.zeros_like(q)
    return [jnp.where(lane < DIFF_DH, q, zero), jnp.where(lane < DIFF_DH, zero, q)]


def _diff_lambda(lam_ref, lam_init):
    lp = lam_ref[...]
    return (jnp.exp(jnp.sum(lp[0:1] * lp[1:2], axis=-1, keepdims=True))
            - jnp.exp(jnp.sum(lp[2:3] * lp[3:4], axis=-1, keepdims=True)) + lam_init)


def _diff_out(accs, ls, hd, lam, g_ref, lam_init):
    od = (accs[2 * hd] / ls[2 * hd] - lam * (accs[2 * hd + 1] / ls[2 * hd + 1])).T
    return (_rms(od, g_ref[...]) * (1.0 - lam_init)).astype(BF16)


def _diff_meta_operands(km_ref, vtm_ref, heads):
    meta_k = [km_ref[0, :, (c // 2) * LANES:(c // 2 + 1) * LANES] for c in range(2 * heads)]
    meta_vt = [vtm_ref[0, 0, (c // 2) * DIFF_V:(c // 2 + 1) * DIFF_V, :] for c in range(2 * heads)]
    return meta_k, meta_vt


def _diff_kernel(q_ref, k_ref, vt_ref, km_ref, vtm_ref, lam_ref, g_ref, o_ref, acc_ref, accm_ref, s_ref,
                 *, t, nq, heads, lam_init):
    def q_of(qi):
        off = pl.multiple_of(qi * t, t)
        qs = []
        for hd in range(heads):
            qs += _diff_queries(q_ref[0, pl.ds(off, t), hd * LANES:(hd + 1) * LANES])
        return qs

    def k_of(j):
        off = pl.multiple_of(j * t, t)
        ks = [k_ref[0, pl.ds(off, t), hd * LANES:(hd + 1) * LANES] for hd in range(heads)]
        return [ks[c // 2] for c in range(2 * heads)]

    def vt_of(j):
        vts = [vt_ref[0, j, hd * DIFF_V:(hd + 1) * DIFF_V, :] for hd in range(heads)]
        return [vts[c // 2] for c in range(2 * heads)]

    lam = _diff_lambda(lam_ref, lam_init)

    def write_out(qi, accs, ls):
        off = pl.multiple_of(qi * t, t)
        for hd in range(heads):
            o_ref[0, pl.ds(off, t), hd * DIFF_V:(hd + 1) * DIFF_V] = _diff_out(
                accs, ls, hd, lam, g_ref, lam_init)

    meta_k, meta_vt = _diff_meta_operands(km_ref, vtm_ref, heads)
    _attend_query_tiles(q_of, k_of, vt_of, meta_k, meta_vt, write_out, nq, t, 2 * heads, acc_ref, accm_ref, s_ref)


def _diff_meta_kernel(q_ref, km_ref, vtm_ref, lam_ref, g_ref, o_ref, *, heads, lam_init):
    tm = q_ref.shape[1]
    causal = (lax.broadcasted_iota(jnp.int32, (META_KEYS, tm), 0)
              <= lax.broadcasted_iota(jnp.int32, (META_KEYS, tm), 1))
    qs = []
    for hd in range(heads):
        qs += _diff_queries(q_ref[0, :, hd * LANES:(hd + 1) * LANES])
    meta_k, meta_vt = _diff_meta_operands(km_ref, vtm_ref, heads)
    _, ls, accs = _attend_meta(qs, meta_k, meta_vt, causal)
    lam = _diff_lambda(lam_ref, lam_init)
    for hd in range(heads):
        o_ref[0, :, hd * DIFF_V:(hd + 1) * DIFF_V] = _diff_out(accs, ls, hd, lam, g_ref, lam_init)


def _diff_call(q, k, vt, km, vtm, lam_p, g_sub, t, lam_init):
    b, l, _ = q.shape
    nq = l // t
    hs = DIFF_HEADS_PER_STEP
    tm = km.shape[1]
    return pl.pallas_call(
        functools.partial(_diff_kernel, t=t, nq=nq, heads=hs, lam_init=lam_init),
        grid=(b, DIFF_HEADS // hs),
        in_specs=[
            pl.BlockSpec((1, l, hs * LANES), lambda bi, hg: (bi, 0, hg)),
            pl.BlockSpec((1, l, hs * LANES), lambda bi, hg: (bi, 0, hg)),
            pl.BlockSpec((1, nq, hs * DIFF_V, t), lambda bi, hg: (bi, 0, hg, 0)),
            pl.BlockSpec((1, tm, hs * LANES), lambda bi, hg: (0, 0, hg)),
            pl.BlockSpec((1, 1, hs * DIFF_V, tm), lambda bi, hg: (0, 0, hg, 0)),
            _const_spec(lam_p.shape), _const_spec(g_sub.shape),
        ],
        out_specs=pl.BlockSpec((1, l, hs * DIFF_V), lambda bi, hg: (bi, 0, hg)),
        out_shape=jax.ShapeDtypeStruct((b, l, DIFF_W), BF16),
        scratch_shapes=[pltpu.VMEM((2 * hs, DIFF_V, t), F32), pltpu.VMEM((2 * hs, DIFF_V, t), F32),
                        pltpu.VMEM((2, 2 * hs, t, t), F32)],
        compiler_params=_params(2),
        name="diff_attn",
    )(q, k, vt, km, vtm, lam_p, g_sub)


def _diff_meta_call(qm, km, vtm, lam_p, g_sub, lam_init):
    _, tm, _ = qm.shape
    hs = DIFF_HEADS_PER_STEP
    return pl.pallas_call(
        functools.partial(_diff_meta_kernel, heads=hs, lam_init=lam_init),
        grid=(DIFF_HEADS // hs,),
        in_specs=[
            pl.BlockSpec((1, tm, hs * LANES), lambda hg: (0, 0, hg)),
            pl.BlockSpec((1, tm, hs * LANES), lambda hg: (0, 0, hg)),
            pl.BlockSpec((1, 1, hs * DIFF_V, tm), lambda hg: (0, 0, hg, 0)),
            _const_spec(lam_p.shape), _const_spec(g_sub.shape),
        ],
        out_specs=pl.BlockSpec((1, tm, hs * DIFF_V), lambda hg: (0, 0, hg)),
        out_shape=jax.ShapeDtypeStruct((1, tm, DIFF_W), BF16),
        compiler_params=_params(1),
        name="diff_meta_attn",
    )(qm, km, vtm, lam_p, g_sub)


def _merge_kernel(r_ref, oa_ref, od_ref, g_pre_ref, wg_ref, woa_ref, wod_ref, wout_ref, g_post_ref, o_ref):
    r = r_ref[...]
    d = r.shape[-1]
    h = _rms(r, g_pre_ref[...]).astype(BF16)
    g = _dot(h, wg_ref[...])
    a = _dot(oa_ref[...], woa_ref[...])
    bb = _dot(od_ref[...], wod_ref[...])
    mixed = (jax.nn.sigmoid(g[:, :d]) * a + jax.nn.sigmoid(g[:, d:]) * bb).astype(BF16)
    mix = _dot(mixed, wout_ref[...])
    o_ref[...] = r + _rms(mix, g_post_ref[...])


def _merge_call(r2, oa2, od2, g_pre, wg, woa, wod, wout, g_post, tm):
    n, d = r2.shape
    row = lambda w: pl.BlockSpec((tm, w), lambda i: (i, 0))
    return pl.pallas_call(
        _merge_kernel,
        grid=(n // tm,),
        in_specs=[row(d), row(oa2.shape[1]), row(od2.shape[1]),
                  _const_spec(g_pre.shape), _const_spec(wg.shape), _const_spec(woa.shape),
                  _const_spec(wod.shape), _const_spec(wout.shape), _const_spec(g_post.shape)],
        out_specs=row(d),
        out_shape=jax.ShapeDtypeStruct((n, d), F32),
        compiler_params=_params(1),
        name="merge",
    )(r2, oa2, od2, g_pre, wg, woa, wod, wout, g_post)


def _ffn_kernel(r_ref, g_pre_ref, wgu_ref, wd_ref, g_post_ref, o_ref):
    r = r_ref[...]
    dff = wd_ref.shape[0]
    h = _rms(r, g_pre_ref[...]).astype(BF16)
    gu = _dot(h, wgu_ref[...])
    gate = gu[:, :dff]
    act = (gate * jax.nn.sigmoid(gate) * gu[:, dff:]).astype(BF16)
    ff = _dot(act, wd_ref[...])
    o_ref[...] = r + _rms(ff, g_post_ref[...])


def _ffn_call(r2, g_pre, wgu, wd, g_post, tm):
    n, d = r2.shape
    row = pl.BlockSpec((tm, d), lambda i: (i, 0))
    return pl.pallas_call(
        _ffn_kernel,
        grid=(n // tm,),
        in_specs=[row, _const_spec(g_pre.shape), _const_spec(wgu.shape),
                  _const_spec(wd.shape), _const_spec(g_post.shape)],
        out_specs=row,
        out_shape=jax.ShapeDtypeStruct((n, d), F32),
        compiler_params=_params(1),
        name="ffn",
    )(r2, g_pre, wgu, wd, g_post)


def _rope_tables(pos, rot_dim, group_starts, scale):
    half = rot_dim // 2
    lane = jnp.arange(LANES)
    first = jnp.zeros((LANES,), bool)
    second = jnp.zeros((LANES,), bool)
    for g in group_starts:
        first = first | ((lane >= g) & (lane < g + half))
        second = second | ((lane >= g + half) & (lane < g + rot_dim))
    freq = jnp.where(first | second, (lane % half).astype(F32), 0.0)
    inv = ROPE_THETA ** (-(2.0 * freq) / rot_dim)
    ang = pos.astype(F32)[:, None] * inv[None, :]
    cos, sin = jnp.cos(ang), jnp.sin(ang)
    c = jnp.where(first | second, cos, 1.0)
    sa = jnp.where(first, -sin, 0.0)
    sb = jnp.where(second, sin, 0.0)
    return jnp.stack([c, sa, sb]) * scale


def _pick_tile(n, candidates):
    for c in candidates:
        if n % c == 0:
            return c
    raise ValueError(f"no tile in {candidates} divides {n}")


def kernel(x, meta_tokens, norm_mix_pre, norm_mix_post, norm_ffn_pre, norm_ffn_post, w_in, mla_q_norm, mla_kv_norm, mla_w_uq, mla_w_ukv, w_o_mla, diff_lambda, diff_subln, w_o_diff, w_out, w_gate_up, w_down):
    dt = x.dtype
    b, s, d = x.shape
    depth = w_in.shape[0]
    r_x = x
    r_m = jnp.zeros((1, META_BLOCK, d), dt).at[0, :N_META].set(meta_tokens.astype(dt))

    t_seq = _pick_tile(s, (512, 256, 128))
    t_tok = _pick_tile(b * s, (512, 256, 128))

    mla_scale = LOG2E / math.sqrt(MLA_NOPE + MLA_ROPE)
    diff_scale = LOG2E / math.sqrt(DIFF_DH)

    def tables(pos):
        return (_rope_tables(pos, MLA_ROPE, (MLA_NOPE,), mla_scale),
                _rope_tables(pos, MLA_ROPE, (MLA_NOPE,), 1.0),
                _rope_tables(pos, DIFF_ROPE, (0, DIFF_DH), diff_scale),
                _rope_tables(pos, DIFF_ROPE, (0, DIFF_DH), 1.0))

    tabs_x = tables(N_META + jnp.arange(s))
    tabs_m = tables(jnp.arange(META_BLOCK))

    c0 = MLA_Q_RANK
    c1 = c0 + MLA_KV_RANK
    c2 = c1 + MLA_ROPE
    c_dv = c2 + 2 * DIFF_W
    c3 = c_dv + DIFF_W
    row1 = lambda v: v.reshape(1, -1).astype(F32)

    for layer in range(depth):
        wl = w_in[layer]
        kpe_w = jnp.zeros((d, LANES), dt).at[:, MLA_NOPE:MLA_NOPE + MLA_ROPE].set(wl[:, c1:c2])
        w1 = jnp.concatenate([wl[:, :c1], kpe_w, wl[:, c2:c_dv]], axis=1).astype(BF16)
        wdvt = wl[:, c_dv:c3].T.astype(BF16)
        wg = wl[:, c3:].astype(BF16)
        wuq = jnp.pad(mla_w_uq[layer].reshape(MLA_Q_RANK, MLA_HEADS, MLA_NOPE + MLA_ROPE),
                      ((0, 0), (0, 0), (0, LANES - MLA_NOPE - MLA_ROPE))
                      ).reshape(MLA_Q_RANK, MLA_W).astype(BF16)
        wukv = mla_w_ukv[layer].reshape(MLA_KV_RANK, MLA_HEADS, MLA_NOPE + MLA_V)
        wk = jnp.pad(wukv[:, :, :MLA_NOPE], ((0, 0), (0, 0), (0, LANES - MLA_NOPE))
                     ).reshape(MLA_KV_RANK, MLA_W).astype(BF16)
        wvt = wukv[:, :, MLA_NOPE:].reshape(MLA_KV_RANK, MLA_HEADS * MLA_V).T.astype(BF16)
        proj_w = (row1(norm_mix_pre[layer]), w1, wdvt, row1(mla_q_norm[layer]), wuq,
                  row1(mla_kv_norm[layer]), wk, wvt)
        merge_w = (row1(norm_mix_pre[layer]), wg, w_o_mla[layer].astype(BF16),
                   w_o_diff[layer].astype(BF16), w_out[layer].astype(BF16), row1(norm_mix_post[layer]))
        ffn_w = (row1(norm_ffn_pre[layer]), w_gate_up[layer].astype(BF16),
                 w_down[layer].astype(BF16), row1(norm_ffn_post[layer]))
        lam_p, g_sub = diff_lambda[layer].astype(F32), row1(diff_subln[layer])
        lam_init = 0.8 - 0.6 * math.exp(-0.3 * layer)

        q, k, vt, dq, dk, dvt = _proj_call(r_x, *proj_w, *tabs_x, t_seq)
        qm, km, vtm, dqm, dkm, dvtm = _proj_call(r_m, *proj_w, *tabs_m, META_BLOCK)

        out_a = _mla_call(q, k, vt, km, vtm, t_seq)
        out_d = _diff_call(dq, dk, dvt, dkm, dvtm, lam_p, g_sub, t_seq, lam_init)
        r2 = _merge_call(r_x.reshape(b * s, d), out_a.reshape(b * s, -1), out_d.reshape(b * s, -1),
                         *merge_w, t_tok)
        r_x = _ffn_call(r2, *ffn_w, t_tok).reshape(b, s, d)

        if layer < depth - 1:
            out_a_m = _mla_meta_call(qm, km, vtm)
            out_d_m = _diff_meta_call(dqm, dkm, dvtm, lam_p, g_sub, lam_init)
            r2 = _merge_call(r_m[0], out_a_m[0], out_d_m[0], *merge_w, META_BLOCK)
            r_m = _ffn_call(r2, *ffn_w, META_BLOCK)[None]

    return r_x
```

```python
import functools
import math

import jax
import jax.numpy as jnp
from jax import lax
from jax.experimental import pallas as pl
from jax.experimental.pallas import tpu as pltpu

N_META = 16
META_BLOCK = 128
META_KEYS = 16
ROPE_THETA = 500000.0
RMS_EPS = 1e-6

MLA_HEADS = 8
MLA_NOPE = 64
MLA_ROPE = 32
MLA_V = 64
MLA_Q_RANK = 384
MLA_KV_RANK = 256

DIFF_HEADS = 4
DIFF_DH = 64
DIFF_ROPE = DIFF_DH // 4
DIFF_V = 2 * DIFF_DH

LANES = 128
MLA_W = MLA_HEADS * LANES
DIFF_W = DIFF_HEADS * DIFF_V
MLA_HEADS_PER_STEP = 4
PROJ_ROW_GROUPS = 2
DIFF_HEADS_PER_STEP = 2
VMEM_LIMIT_BYTES = 56 * 1024 * 1024
MASK_VALUE = -1e30
LOG2E = 1.4426950408889634

BF16 = jnp.bfloat16
F32 = jnp.float32


def _rms(x, gain):
    ms = jnp.mean(x * x, axis=-1, keepdims=True)
    return x * lax.rsqrt(ms + RMS_EPS) * gain


def _rope(x, c, sa, sb, half):
    return x * c + pltpu.roll(x, LANES - half, 1) * sa + pltpu.roll(x, half, 1) * sb


def _dot(a, b):
    return jnp.dot(a, b, preferred_element_type=F32)


def _dot_nt(a, b):
    return lax.dot_general(a, b, (((1,), (1,)), ((), ())), preferred_element_type=F32)


def _const_spec(shape):
    nd = len(shape)
    return pl.BlockSpec(shape, lambda *_: (0,) * nd, pipeline_mode=pl.Buffered(1))


def _params(n_grid):
    return pltpu.CompilerParams(
        dimension_semantics=("arbitrary",) * n_grid,
        vmem_limit_bytes=VMEM_LIMIT_BYTES,
    )


def _proj_kernel(r_ref, g_pre_ref, w1_ref, wdvt_ref, g_q_ref, wuq_ref, g_kv_ref, wk_ref, wvt_ref,
                 tk_ref, td_ref,
                 q_ref, k_ref, vt_ref, dq_ref, dk_ref, dvt_ref, *, mla_scale, diff_scale):
    tm = r_ref.shape[1]
    ng = PROJ_ROW_GROUPS if tm % (PROJ_ROW_GROUPS * LANES) == 0 else 1
    gr = tm // ng
    for gi in range(ng):
        rs = slice(gi * gr, (gi + 1) * gr)
        r = r_ref[0, rs, :]
        h = _rms(r, g_pre_ref[...]).astype(BF16)
        p1 = _dot(h, w1_ref[...])
        o = 0
        q_lat = p1[:, o:o + MLA_Q_RANK]; o += MLA_Q_RANK
        kv_lat = p1[:, o:o + MLA_KV_RANK]; o += MLA_KV_RANK
        kpe = p1[:, o:o + LANES]; o += LANES
        dq = p1[:, o:o + DIFF_W]; o += DIFF_W
        dk = p1[:, o:o + DIFF_W]

        cq = _rms(q_lat, g_q_ref[...]).astype(BF16)
        qa = _dot(cq, wuq_ref[...])
        ct, sat, sbt = tk_ref[0, rs, :], tk_ref[1, rs, :], tk_ref[2, rs, :]
        for hd in range(MLA_HEADS):
            sl = slice(hd * LANES, (hd + 1) * LANES)
            q_ref[0, rs, sl] = (_rope(qa[:, sl], ct, sat, sbt, MLA_ROPE // 2) * mla_scale).astype(BF16)

        ckv = _rms(kv_lat, g_kv_ref[...]).astype(BF16)
        kn = _dot(ckv, wk_ref[...])
        kpe_r = _rope(kpe, ct, sat, sbt, MLA_ROPE // 2)
        for hd in range(MLA_HEADS):
            sl = slice(hd * LANES, (hd + 1) * LANES)
            k_ref[0, rs, sl] = (kn[:, sl] + kpe_r).astype(BF16)
        vt_ref[0, 0, :, rs] = _dot_nt(wvt_ref[...], ckv).astype(BF16)

        for hd in range(DIFF_HEADS):
            sl = slice(hd * LANES, (hd + 1) * LANES)
            cd, sad, sbd = td_ref[0, rs, :], td_ref[1, rs, :], td_ref[2, rs, :]
            dq_ref[0, rs, sl] = (_rope(dq[:, sl], cd, sad, sbd, DIFF_ROPE // 2) * diff_scale).astype(BF16)
            dk_ref[0, rs, sl] = _rope(dk[:, sl], cd, sad, sbd, DIFF_ROPE // 2).astype(BF16)
        dvt_ref[0, 0, :, rs] = _dot_nt(wdvt_ref[...], h).astype(BF16)


def _proj_call(r, g_pre, w1, wdvt, g_q, wuq, g_kv, wk, wvt, tk, td, tm, mla_scale, diff_scale):
    b, l, d = r.shape
    nt = l // tm
    tab_spec = pl.BlockSpec((3, tm, LANES), lambda ti, bi: (0, ti, 0))

    def rows(n):
        return pl.BlockSpec((1, tm, n), lambda ti, bi: (bi, ti, 0))

    def cols(n):
        return pl.BlockSpec((1, 1, n, tm), lambda ti, bi: (bi, ti, 0, 0))

    nv, ndv = wvt.shape[0], wdvt.shape[0]
    return pl.pallas_call(
        functools.partial(_proj_kernel, mla_scale=mla_scale, diff_scale=diff_scale),
        grid=(nt, b),
        in_specs=[
            rows(d),
            _const_spec(g_pre.shape), _const_spec(w1.shape), _const_spec(wdvt.shape),
            _const_spec(g_q.shape), _const_spec(wuq.shape),
            _const_spec(g_kv.shape), _const_spec(wk.shape), _const_spec(wvt.shape),
            tab_spec, tab_spec,
        ],
        out_specs=[rows(MLA_W), rows(MLA_W), cols(nv), rows(DIFF_W), rows(DIFF_W), cols(ndv)],
        out_shape=[
            jax.ShapeDtypeStruct((b, l, MLA_W), BF16),
            jax.ShapeDtypeStruct((b, l, MLA_W), BF16),
            jax.ShapeDtypeStruct((b, nt, nv, tm), BF16),
            jax.ShapeDtypeStruct((b, l, DIFF_W), BF16),
            jax.ShapeDtypeStruct((b, l, DIFF_W), BF16),
            jax.ShapeDtypeStruct((b, nt, ndv, tm), BF16),
        ],
        compiler_params=_params(2),
        name="proj",
    )(r, g_pre, w1, wdvt, g_q, wuq, g_kv, wk, wvt, tk, td)


def _attend_meta(qs, meta_k, meta_vt, causal):
    tm, tq = meta_k[0].shape[0], qs[0].shape[0]
    valid = lax.broadcasted_iota(jnp.int32, (META_KEYS, tq), 0) < N_META
    if causal is not None:
        valid = jnp.logical_and(valid, causal)
    ss = [_dot_nt(meta_k[c][:META_KEYS], qs[c]) for c in range(len(qs))]
    ms, ls, ps = [], [], []
    for s in ss:
        if causal is not None or META_KEYS != N_META:
            s = jnp.where(valid, s, MASK_VALUE)
        m = jnp.max(s, axis=0, keepdims=True)
        p = jnp.exp2(s - m)
        ms.append(m)
        ls.append(jnp.sum(p, axis=0, keepdims=True))
        ps.append(jnp.concatenate([p.astype(BF16), jnp.zeros((tm - META_KEYS, tq), BF16)], axis=0))
    accs = [_dot(meta_vt[c], ps[c]) for c in range(len(qs))]
    return tuple(ms), tuple(ls), accs


def _attend_query_tiles(q_of, k_of, vt_of, meta_k, meta_vt, write_out, nq, t, n, acc_ref, accm_ref, s_ref):
    key = lax.broadcasted_iota(jnp.int32, (t, t), 0)
    qry = lax.broadcasted_iota(jnp.int32, (t, t), 1)
    diag = key <= qry

    def matmul_scores(qi, j):
        qs, ks = q_of(qi), k_of(j)
        return [_dot_nt(ks[c], qs[c]) for c in range(n)]

    def park_scores(ss, slot, mask):
        tile_max = []
        for c in range(n):
            s = ss[c] if mask is None else jnp.where(mask, ss[c], MASK_VALUE)
            s_ref[slot, c] = s
            tile_max.append(jnp.max(s, axis=0, keepdims=True))
        return tuple(tile_max)

    def update(j, slot, tile_max, ms, ls):
        vts = vt_of(j)
        ms_new, ls_new = [], []
        for c in range(n):
            m_new = jnp.maximum(ms[c], tile_max[c])
            alpha = jnp.exp2(ms[c] - m_new)
            p = jnp.exp2(s_ref[slot, c] - m_new)
            ls_new.append(alpha * ls[c] + jnp.sum(p, axis=0, keepdims=True))
            acc_ref[c] = alpha * acc_ref[c] + _dot(vts[c], p.astype(BF16))
            ms_new.append(m_new)
        return tuple(ms_new), tuple(ls_new)

    def start_query_tile(qi, mask):
        ss = matmul_scores(qi, 0)

        def rest():
            ms0, ls0, accs0 = _attend_meta(q_of(qi), meta_k, meta_vt, None)
            for c in range(n):
                accm_ref[c] = accs0[c]
            return park_scores(ss, 0, mask), ms0, ls0

        return rest

    def query_tile(qi, first):
        first_max, ms0, ls0 = first
        acc_ref[...] = accm_ref[...]

        def step(j, slot, carry, mask):
            tile_max, ms, ls = carry
            nxt = park_scores(matmul_scores(qi, j + 1), 1 - slot, mask)
            ms, ls = update(j, slot, tile_max, ms, ls)
            return nxt, ms, ls

        def pair(i, carry):
            carry = step(2 * i, 0, carry, None)
            return step(2 * i + 1, 1, carry, None)

        def last(slot, carry):
            nxt_first = start_query_tile(jnp.minimum(qi + 1, nq - 1), None)
            _, ls = update(qi, slot, *carry)
            return ls, nxt_first()

        def finish_even(carry):
            return last(1, step(qi - 1, 0, carry, diag))

        def finish_odd(carry):
            carry = step(qi - 2, 0, carry, None)
            return last(0, step(qi - 1, 1, carry, diag))

        def finish(carry):
            return lax.cond(((qi - 1) & 1) == 0, finish_even, finish_odd, carry)

        def only_diagonal(carry):
            return last(0, carry)

        carry = (first_max, ms0, ls0)
        carry = lax.fori_loop(0, jnp.maximum(qi - 1, 0) // 2, pair, carry)
        ls, next_first = lax.cond(qi >= 1, finish, only_diagonal, carry)
        write_out(qi, [acc_ref[c] for c in range(n)], ls)
        return next_first

    lax.fori_loop(0, nq, query_tile, start_query_tile(0, diag)())


def _mla_out(accs, ls):
    out_t = jnp.concatenate([accs[c] / ls[c] for c in range(len(accs))], axis=0)
    return out_t.T.astype(BF16)


def _mla_meta_operands(km_ref, vtm_ref, heads):
    meta_k = [km_ref[0, :, c * LANES:(c + 1) * LANES] for c in range(heads)]
    meta_vt = [vtm_ref[0, 0, c * MLA_V:(c + 1) * MLA_V, :] for c in range(heads)]
    return meta_k, meta_vt


def _mla_kernel(q_ref, k_ref, vt_ref, km_ref, vtm_ref, o_ref, acc_ref, accm_ref, s_ref, *, t, nq, heads):
    lane = [slice(c * LANES, (c + 1) * LANES) for c in range(heads)]
    rows = [slice(c * MLA_V, (c + 1) * MLA_V) for c in range(heads)]

    def q_of(qi):
        off = pl.multiple_of(qi * t, t)
        return [q_ref[0, pl.ds(off, t), lane[c]] for c in range(heads)]

    def k_of(j):
        off = pl.multiple_of(j * t, t)
        return [k_ref[0, pl.ds(off, t), lane[c]] for c in range(heads)]

    def vt_of(j):
        return [vt_ref[0, j, rows[c], :] for c in range(heads)]

    def write_out(qi, accs, ls):
        o_ref[0, pl.ds(pl.multiple_of(qi * t, t), t), :] = _mla_out(accs, ls)

    meta_k, meta_vt = _mla_meta_operands(km_ref, vtm_ref, heads)
    _attend_query_tiles(q_of, k_of, vt_of, meta_k, meta_vt, write_out, nq, t, heads, acc_ref, accm_ref, s_ref)


def _mla_meta_kernel(q_ref, km_ref, vtm_ref, o_ref, *, heads):
    tm = q_ref.shape[1]
    causal = (lax.broadcasted_iota(jnp.int32, (META_KEYS, tm), 0)
              <= lax.broadcasted_iota(jnp.int32, (META_KEYS, tm), 1))
    qs = [q_ref[0, :, c * LANES:(c + 1) * LANES] for c in range(heads)]
    meta_k, meta_vt = _mla_meta_operands(km_ref, vtm_ref, heads)
    _, ls, accs = _attend_meta(qs, meta_k, meta_vt, causal)
    o_ref[0] = _mla_out(accs, ls)


def _mla_call(q, k, vt, km, vtm, t):
    b, l, _ = q.shape
    nq = l // t
    hs = MLA_HEADS_PER_STEP
    tm = km.shape[1]
    return pl.pallas_call(
        functools.partial(_mla_kernel, t=t, nq=nq, heads=hs),
        grid=(b, MLA_HEADS // hs),
        in_specs=[
            pl.BlockSpec((1, l, hs * LANES), lambda bi, hg: (bi, 0, hg)),
            pl.BlockSpec((1, l, hs * LANES), lambda bi, hg: (bi, 0, hg)),
            pl.BlockSpec((1, nq, hs * MLA_V, t), lambda bi, hg: (bi, 0, hg, 0)),
            pl.BlockSpec((1, tm, hs * LANES), lambda bi, hg: (0, 0, hg)),
            pl.BlockSpec((1, 1, hs * MLA_V, tm), lambda bi, hg: (0, 0, hg, 0)),
        ],
        out_specs=pl.BlockSpec((1, l, hs * MLA_V), lambda bi, hg: (bi, 0, hg)),
        out_shape=jax.ShapeDtypeStruct((b, l, MLA_HEADS * MLA_V), BF16),
        scratch_shapes=[pltpu.VMEM((hs, MLA_V, t), F32), pltpu.VMEM((hs, MLA_V, t), F32),
                        pltpu.VMEM((2, hs, t, t), F32)],
        compiler_params=_params(2),
        name="mla_attn",
    )(q, k, vt, km, vtm)


def _mla_meta_call(qm, km, vtm):
    _, tm, _ = qm.shape
    hs = MLA_HEADS_PER_STEP
    return pl.pallas_call(
        functools.partial(_mla_meta_kernel, heads=hs),
        grid=(MLA_HEADS // hs,),
        in_specs=[
            pl.BlockSpec((1, tm, hs * LANES), lambda hg: (0, 0, hg)),
            pl.BlockSpec((1, tm, hs * LANES), lambda hg: (0, 0, hg)),
            pl.BlockSpec((1, 1, hs * MLA_V, tm), lambda hg: (0, 0, hg, 0)),
        ],
        out_specs=pl.BlockSpec((1, tm, hs * MLA_V), lambda hg: (0, 0, hg)),
        out_shape=jax.ShapeDtypeStruct((1, tm, MLA_HEADS * MLA_V), BF16),
        compiler_params=_params(1),
        name="mla_meta_attn",
    )(qm, km, vtm)


def _diff_queries(q):
    lane = lax.broadcasted_iota(jnp.int32, q.shape, 1)
    zero = jnp.zeros_like(q)
    return [jnp.where(lane < DIFF_DH, q, zero), jnp.where(lane < DIFF_DH, zero, q)]


def _diff_lambda(lam_ref, lam_init):
    lp = lam_ref[...]
    return (jnp.exp(jnp.sum(lp[0:1] * lp[1:2], axis=-1, keepdims=True))
            - jnp.exp(jnp.sum(lp[2:3] * lp[3:4], axis=-1, keepdims=True)) + lam_init)


def _diff_out(accs, ls, hd, lam, g_ref, lam_init):
    od = (accs[2 * hd] / ls[2 * hd] - lam * (accs[2 * hd + 1] / ls[2 * hd + 1])).T
    return (_rms(od, g_ref[...]) * (1.0 - lam_init)).astype(BF16)


def _diff_meta_operands(km_ref, vtm_ref, heads):
    meta_k = [km_ref[0, :, (c // 2) * LANES:(c // 2 + 1) * LANES] for c in range(2 * heads)]
    meta_vt = [vtm_ref[0, 0, (c // 2) * DIFF_V:(c // 2 + 1) * DIFF_V, :] for c in range(2 * heads)]
    return meta_k, meta_vt


def _diff_kernel(q_ref, k_ref, vt_ref, km_ref, vtm_ref, lam_ref, g_ref, o_ref, acc_ref, accm_ref, s_ref,
                 *, t, nq, heads, lam_init):
    def q_of(qi):
        off = pl.multiple_of(qi * t, t)
        qs = []
        for hd in range(heads):
            qs += _diff_queries(q_ref[0, pl.ds(off, t), hd * LANES:(hd + 1) * LANES])
        return qs

    def k_of(j):
        off = pl.multiple_of(j * t, t)
        ks = [k_ref[0, pl.ds(off, t), hd * LANES:(hd + 1) * LANES] for hd in range(heads)]
        return [ks[c // 2] for c in range(2 * heads)]

    def vt_of(j):
        vts = [vt_ref[0, j, hd * DIFF_V:(hd + 1) * DIFF_V, :] for hd in range(heads)]
        return [vts[c // 2] for c in range(2 * heads)]

    lam = _diff_lambda(lam_ref, lam_init)

    def write_out(qi, accs, ls):
        off = pl.multiple_of(qi * t, t)
        for hd in range(heads):
            o_ref[0, pl.ds(off, t), hd * DIFF_V:(hd + 1) * DIFF_V] = _diff_out(
                accs, ls, hd, lam, g_ref, lam_init)

    meta_k, meta_vt = _diff_meta_operands(km_ref, vtm_ref, heads)
    _attend_query_tiles(q_of, k_of, vt_of, meta_k, meta_vt, write_out, nq, t, 2 * heads, acc_ref, accm_ref, s_ref)


def _diff_meta_kernel(q_ref, km_ref, vtm_ref, lam_ref, g_ref, o_ref, *, heads, lam_init):
    tm = q_ref.shape[1]
    causal = (lax.broadcasted_iota(jnp.int32, (META_KEYS, tm), 0)
              <= lax.broadcasted_iota(jnp.int32, (META_KEYS, tm), 1))
    qs = []
    for hd in range(heads):
        qs += _diff_queries(q_ref[0, :, hd * LANES:(hd + 1) * LANES])
    meta_k, meta_vt = _diff_meta_operands(km_ref, vtm_ref, heads)
    _, ls, accs = _attend_meta(qs, meta_k, meta_vt, causal)
    lam = _diff_lambda(lam_ref, lam_init)
    for hd in range(heads):
        o_ref[0, :, hd * DIFF_V:(hd + 1) * DIFF_V] = _diff_out(accs, ls, hd, lam, g_ref, lam_init)


def _diff_call(q, k, vt, km, vtm, lam_p, g_sub, t, lam_init):
    b, l, _ = q.shape
    nq = l // t
    hs = DIFF_HEADS_PER_STEP
    tm = km.shape[1]
    return pl.pallas_call(
        functools.partial(_diff_kernel, t=t, nq=nq, heads=hs, lam_init=lam_init),
        grid=(b, DIFF_HEADS // hs),
        in_specs=[
            pl.BlockSpec((1, l, hs * LANES), lambda bi, hg: (bi, 0, hg)),
            pl.BlockSpec((1, l, hs * LANES), lambda bi, hg: (bi, 0, hg)),
            pl.BlockSpec((1, nq, hs * DIFF_V, t), lambda bi, hg: (bi, 0, hg, 0)),
            pl.BlockSpec((1, tm, hs * LANES), lambda bi, hg: (0, 0, hg)),
            pl.BlockSpec((1, 1, hs * DIFF_V, tm), lambda bi, hg: (0, 0, hg, 0)),
            _const_spec(lam_p.shape), _const_spec(g_sub.shape),
        ],
        out_specs=pl.BlockSpec((1, l, hs * DIFF_V), lambda bi, hg: (bi, 0, hg)),
        out_shape=jax.ShapeDtypeStruct((b, l, DIFF_W), BF16),
        scratch_shapes=[pltpu.VMEM((2 * hs, DIFF_V, t), F32), pltpu.VMEM((2 * hs, DIFF_V, t), F32),
                        pltpu.VMEM((2, 2 * hs, t, t), F32)],
        compiler_params=_params(2),
        name="diff_attn",
    )(q, k, vt, km, vtm, lam_p, g_sub)


def _diff_meta_call(qm, km, vtm, lam_p, g_sub, lam_init):
    _, tm, _ = qm.shape
    hs = DIFF_HEADS_PER_STEP
    return pl.pallas_call(
        functools.partial(_diff_meta_kernel, heads=hs, lam_init=lam_init),
        grid=(DIFF_HEADS // hs,),
        in_specs=[
            pl.BlockSpec((1, tm, hs * LANES), lambda hg: (0, 0, hg)),
            pl.BlockSpec((1, tm, hs * LANES), lambda hg: (0, 0, hg)),
            pl.BlockSpec((1, 1, hs * DIFF_V, tm), lambda hg: (0, 0, hg, 0)),
            _const_spec(lam_p.shape), _const_spec(g_sub.shape),
        ],
        out_specs=pl.BlockSpec((1, tm, hs * DIFF_V), lambda hg: (0, 0, hg)),
        out_shape=jax.ShapeDtypeStruct((1, tm, DIFF_W), BF16),
        compiler_params=_params(1),
        name="diff_meta_attn",
    )(qm, km, vtm, lam_p, g_sub)


def _merge_kernel(r_ref, oa_ref, od_ref, g_pre_ref, wg_ref, woa_ref, wod_ref, wout_ref, g_post_ref, o_ref):
    r = r_ref[...]
    d = r.shape[-1]
    h = _rms(r, g_pre_ref[...]).astype(BF16)
    g = _dot(h, wg_ref[...])
    a = _dot(oa_ref[...], woa_ref[...])
    bb = _dot(od_ref[...], wod_ref[...])
    mixed = (jax.nn.sigmoid(g[:, :d]) * a + jax.nn.sigmoid(g[:, d:]) * bb).astype(BF16)
    mix = _dot(mixed, wout_ref[...])
    o_ref[...] = r + _rms(mix, g_post_ref[...])


def _merge_call(r2, oa2, od2, g_pre, wg, woa, wod, wout, g_post, tm):
    n, d = r2.shape
    row = lambda w: pl.BlockSpec((tm, w), lambda i: (i, 0))
    return pl.pallas_call(
        _merge_kernel,
        grid=(n // tm,),
        in_specs=[row(d), row(oa2.shape[1]), row(od2.shape[1]),
                  _const_spec(g_pre.shape), _const_spec(wg.shape), _const_spec(woa.shape),
                  _const_spec(wod.shape), _const_spec(wout.shape), _const_spec(g_post.shape)],
        out_specs=row(d),
        out_shape=jax.ShapeDtypeStruct((n, d), F32),
        compiler_params=_params(1),
        name="merge",
    )(r2, oa2, od2, g_pre, wg, woa, wod, wout, g_post)


def _ffn_kernel(r_ref, g_pre_ref, wgu_ref, wd_ref, g_post_ref, o_ref):
    r = r_ref[...]
    dff = wd_ref.shape[0]
    h = _rms(r, g_pre_ref[...]).astype(BF16)
    gu = _dot(h, wgu_ref[...])
    gate = gu[:, :dff]
    act = (gate * jax.nn.sigmoid(gate) * gu[:, dff:]).astype(BF16)
    ff = _dot(act, wd_ref[...])
    o_ref[...] = r + _rms(ff, g_post_ref[...])


def _ffn_call(r2, g_pre, wgu, wd, g_post, tm):
    n, d = r2.shape
    row = pl.BlockSpec((tm, d), lambda i: (i, 0))
    return pl.pallas_call(
        _ffn_kernel,
        grid=(n // tm,),
        in_specs=[row, _const_spec(g_pre.shape), _const_spec(wgu.shape),
                  _const_spec(wd.shape), _const_spec(g_post.shape)],
        out_specs=row,
        out_shape=jax.ShapeDtypeStruct((n, d), F32),
        compiler_params=_params(1),
        name="ffn",
    )(r2, g_pre, wgu, wd, g_post)


def _rope_tables(pos, rot_dim, group_starts):
    half = rot_dim // 2
    lane = jnp.arange(LANES)
    first = jnp.zeros((LANES,), bool)
    second = jnp.zeros((LANES,), bool)
    for g in group_starts:
        first = first | ((lane >= g) & (lane < g + half))
        second = second | ((lane >= g + half) & (lane < g + rot_dim))
    freq = jnp.where(first | second, (lane % half).astype(F32), 0.0)
    inv = ROPE_THETA ** (-(2.0 * freq) / rot_dim)
    ang = pos.astype(F32)[:, None] * inv[None, :]
    cos, sin = jnp.cos(ang), jnp.sin(ang)
    c = jnp.where(first | second, cos, 1.0)
    sa = jnp.where(first, -sin, 0.0)
    sb = jnp.where(second, sin, 0.0)
    return jnp.stack([c, sa, sb])


def _pick_tile(n, candidates):
    for c in candidates:
        if n % c == 0:
            return c
    raise ValueError(f"no tile in {candidates} divides {n}")


def kernel(x, meta_tokens, norm_mix_pre, norm_mix_post, norm_ffn_pre, norm_ffn_post, w_in, mla_q_norm, mla_kv_norm, mla_w_uq, mla_w_ukv, w_o_mla, diff_lambda, diff_subln, w_o_diff, w_out, w_gate_up, w_down):
    dt = x.dtype
    b, s, d = x.shape
    depth = w_in.shape[0]
    r_x = x
    r_m = jnp.zeros((1, META_BLOCK, d), dt).at[0, :N_META].set(meta_tokens.astype(dt))

    t_seq = _pick_tile(s, (512, 256, 128))
    t_tok = _pick_tile(b * s, (512, 256, 128))

    mla_scale = LOG2E / math.sqrt(MLA_NOPE + MLA_ROPE)
    diff_scale = LOG2E / math.sqrt(DIFF_DH)

    def tables(pos):
        return (_rope_tables(pos, MLA_ROPE, (MLA_NOPE,)), _rope_tables(pos, DIFF_ROPE, (0, DIFF_DH)))

    tabs_x = tables(N_META + jnp.arange(s))
    tabs_m = tables(jnp.arange(META_BLOCK))

    c0 = MLA_Q_RANK
    c1 = c0 + MLA_KV_RANK
    c2 = c1 + MLA_ROPE
    c_dv = c2 + 2 * DIFF_W
    c3 = c_dv + DIFF_W
    row1 = lambda v: v.reshape(1, -1).astype(F32)

    for layer in range(depth):
        wl = w_in[layer]
        kpe_w = jnp.zeros((d, LANES), dt).at[:, MLA_NOPE:MLA_NOPE + MLA_ROPE].set(wl[:, c1:c2])
        w1 = jnp.concatenate([wl[:, :c1], kpe_w, wl[:, c2:c_dv]], axis=1).astype(BF16)
        wdvt = wl[:, c_dv:c3].T.astype(BF16)
        wg = wl[:, c3:].astype(BF16)
        wuq = jnp.pad(mla_w_uq[layer].reshape(MLA_Q_RANK, MLA_HEADS, MLA_NOPE + MLA_ROPE),
                      ((0, 0), (0, 0), (0, LANES - MLA_NOPE - MLA_ROPE))
                      ).reshape(MLA_Q_RANK, MLA_W).astype(BF16)
        wukv = mla_w_ukv[layer].reshape(MLA_KV_RANK, MLA_HEADS, MLA_NOPE + MLA_V)
        wk = jnp.pad(wukv[:, :, :MLA_NOPE], ((0, 0), (0, 0), (0, LANES - MLA_NOPE))
                     ).reshape(MLA_KV_RANK, MLA_W).astype(BF16)
        wvt = wukv[:, :, MLA_NOPE:].reshape(MLA_KV_RANK, MLA_HEADS * MLA_V).T.astype(BF16)
        proj_w = (row1(norm_mix_pre[layer]), w1, wdvt, row1(mla_q_norm[layer]), wuq,
                  row1(mla_kv_norm[layer]), wk, wvt)
        merge_w = (row1(norm_mix_pre[layer]), wg, w_o_mla[layer].astype(BF16),
                   w_o_diff[layer].astype(BF16), w_out[layer].astype(BF16), row1(norm_mix_post[layer]))
        ffn_w = (row1(norm_ffn_pre[layer]), w_gate_up[layer].astype(BF16),
                 w_down[layer].astype(BF16), row1(norm_ffn_post[layer]))
        lam_p, g_sub = diff_lambda[layer].astype(F32), row1(diff_subln[layer])
        lam_init = 0.8 - 0.6 * math.exp(-0.3 * layer)

        q, k, vt, dq, dk, dvt = _proj_call(r_x, *proj_w, *tabs_x, t_seq, mla_scale, diff_scale)
        qm, km, vtm, dqm, dkm, dvtm = _proj_call(r_m, *proj_w, *tabs_m, META_BLOCK, mla_scale, diff_scale)

        out_a = _mla_call(q, k, vt, km, vtm, t_seq)
        out_d = _diff_call(dq, dk, dvt, dkm, dvtm, lam_p, g_sub, t_seq, lam_init)
        r2 = _merge_call(r_x.reshape(b * s, d), out_a.reshape(b * s, -1), out_d.reshape(b * s, -1),
                         *merge_w, t_tok)
        r_x = _ffn_call(r2, *ffn_w, t_tok).reshape(b, s, d)

        if layer < depth - 1:
            out_a_m = _mla_meta_call(qm, km, vtm)
            out_d_m = _diff_meta_call(dqm, dkm, dvtm, lam_p, g_sub, lam_init)
            r2 = _merge_call(r_m[0], out_a_m[0], out_d_m[0], *merge_w, META_BLOCK)
            r_m = _ffn_call(r2, *ffn_w, META_BLOCK)[None]

    return r_x
```

```python
import functools
import math

import jax
import jax.numpy as jnp
from jax import lax
from jax.experimental import pallas as pl
from jax.experimental.pallas import tpu as pltpu

N_META = 16
META_BLOCK = 128
META_KEYS = 16
ROPE_THETA = 500000.0
RMS_EPS = 1e-6

MLA_HEADS = 8
MLA_NOPE = 64
MLA_ROPE = 32
MLA_V = 64
MLA_Q_RANK = 384
MLA_KV_RANK = 256

DIFF_HEADS = 4
DIFF_DH = 64
DIFF_ROPE = DIFF_DH // 4
DIFF_V = 2 * DIFF_DH

LANES = 128
MLA_W = MLA_HEADS * LANES
DIFF_W = DIFF_HEADS * DIFF_V
MLA_HEADS_PER_STEP = 4
PROJ_ROW_GROUPS = 2
DIFF_HEADS_PER_STEP = 2
VMEM_LIMIT_BYTES = 56 * 1024 * 1024
MASK_VALUE = -1e30
LOG2E = 1.4426950408889634

BF16 = jnp.bfloat16
F32 = jnp.float32


def _rms(x, gain):
    ms = jnp.mean(x * x, axis=-1, keepdims=True)
    return x * lax.rsqrt(ms + RMS_EPS) * gain


def _rope(x, c, sa, sb, half):
    return x * c + pltpu.roll(x, LANES - half, 1) * sa + pltpu.roll(x, half, 1) * sb


def _dot(a, b):
    return jnp.dot(a, b, preferred_element_type=F32)


def _dot_nt(a, b):
    return lax.dot_general(a, b, (((1,), (1,)), ((), ())), preferred_element_type=F32)


def _const_spec(shape):
    nd = len(shape)
    return pl.BlockSpec(shape, lambda *_: (0,) * nd, pipeline_mode=pl.Buffered(1))


def _params(n_grid):
    return pltpu.CompilerParams(
        dimension_semantics=("arbitrary",) * n_grid,
        vmem_limit_bytes=VMEM_LIMIT_BYTES,
    )


def _proj_kernel(r_ref, g_pre_ref, w1_ref, wdvt_ref, g_q_ref, wuq_ref, g_kv_ref, wk_ref, wvt_ref,
                 tk_ref, td_ref,
                 q_ref, k_ref, vt_ref, dq_ref, dk_ref, dvt_ref, *, mla_scale, diff_scale):
    tm = r_ref.shape[1]
    ng = PROJ_ROW_GROUPS if tm % (PROJ_ROW_GROUPS * LANES) == 0 else 1
    gr = tm // ng
    for gi in range(ng):
        rs = slice(gi * gr, (gi + 1) * gr)
        r = r_ref[0, rs, :]
        h = _rms(r, g_pre_ref[...]).astype(BF16)
        p1 = _dot(h, w1_ref[...])
        o = 0
        q_lat = p1[:, o:o + MLA_Q_RANK]; o += MLA_Q_RANK
        kv_lat = p1[:, o:o + MLA_KV_RANK]; o += MLA_KV_RANK
        kpe = p1[:, o:o + LANES]; o += LANES
        dq = p1[:, o:o + DIFF_W]; o += DIFF_W
        dk = p1[:, o:o + DIFF_W]

        cq = _rms(q_lat, g_q_ref[...]).astype(BF16)
        qa = _dot(cq, wuq_ref[...])
        ct, sat, sbt = tk_ref[0, rs, :], tk_ref[1, rs, :], tk_ref[2, rs, :]
        for hd in range(MLA_HEADS):
            sl = slice(hd * LANES, (hd + 1) * LANES)
            q_ref[0, rs, sl] = (_rope(qa[:, sl], ct, sat, sbt, MLA_ROPE // 2) * mla_scale).astype(BF16)

        ckv = _rms(kv_lat, g_kv_ref[...]).astype(BF16)
        kn = _dot(ckv, wk_ref[...])
        kpe_r = _rope(kpe, ct, sat, sbt, MLA_ROPE // 2)
        for hd in range(MLA_HEADS):
            sl = slice(hd * LANES, (hd + 1) * LANES)
            k_ref[0, rs, sl] = (kn[:, sl] + kpe_r).astype(BF16)
        vt_ref[0, 0, :, rs] = _dot_nt(wvt_ref[...], ckv).astype(BF16)

        for hd in range(DIFF_HEADS):
            sl = slice(hd * LANES, (hd + 1) * LANES)
            cd, sad, sbd = td_ref[0, rs, :], td_ref[1, rs, :], td_ref[2, rs, :]
            dq_ref[0, rs, sl] = (_rope(dq[:, sl], cd, sad, sbd, DIFF_ROPE // 2) * diff_scale).astype(BF16)
            dk_ref[0, rs, sl] = _rope(dk[:, sl], cd, sad, sbd, DIFF_ROPE // 2).astype(BF16)
        dvt_ref[0, 0, :, rs] = _dot_nt(wdvt_ref[...], h).astype(BF16)


def _proj_call(r, g_pre, w1, wdvt, g_q, wuq, g_kv, wk, wvt, tk, td, tm, mla_scale, diff_scale):
    b, l, d = r.shape
    nt = l // tm
    tab_spec = pl.BlockSpec((3, tm, LANES), lambda ti, bi: (0, ti, 0))

    def rows(n):
        return pl.BlockSpec((1, tm, n), lambda ti, bi: (bi, ti, 0))

    def cols(n):
        return pl.BlockSpec((1, 1, n, tm), lambda ti, bi: (bi, ti, 0, 0))

    nv, ndv = wvt.shape[0], wdvt.shape[0]
    return pl.pallas_call(
        functools.partial(_proj_kernel, mla_scale=mla_scale, diff_scale=diff_scale),
        grid=(nt, b),
        in_specs=[
            rows(d),
            _const_spec(g_pre.shape), _const_spec(w1.shape), _const_spec(wdvt.shape),
            _const_spec(g_q.shape), _const_spec(wuq.shape),
            _const_spec(g_kv.shape), _const_spec(wk.shape), _const_spec(wvt.shape),
            tab_spec, tab_spec,
        ],
        out_specs=[rows(MLA_W), rows(MLA_W), cols(nv), rows(DIFF_W), rows(DIFF_W), cols(ndv)],
        out_shape=[
            jax.ShapeDtypeStruct((b, l, MLA_W), BF16),
            jax.ShapeDtypeStruct((b, l, MLA_W), BF16),
            jax.ShapeDtypeStruct((b, nt, nv, tm), BF16),
            jax.ShapeDtypeStruct((b, l, DIFF_W), BF16),
            jax.ShapeDtypeStruct((b, l, DIFF_W), BF16),
            jax.ShapeDtypeStruct((b, nt, ndv, tm), BF16),
        ],
        compiler_params=_params(2),
        name="proj",
    )(r, g_pre, w1, wdvt, g_q, wuq, g_kv, wk, wvt, tk, td)


def _attend_meta(qs, meta_k, meta_vt, causal):
    tm, tq = meta_k[0].shape[0], qs[0].shape[0]
    valid = lax.broadcasted_iota(jnp.int32, (META_KEYS, tq), 0) < N_META
    if causal is not None:
        valid = jnp.logical_and(valid, causal)
    ss = [_dot_nt(meta_k[c][:META_KEYS], qs[c]) for c in range(len(qs))]
    ms, ls, ps = [], [], []
    for s in ss:
        if causal is not None or META_KEYS != N_META:
            s = jnp.where(valid, s, MASK_VALUE)
        m = jnp.max(s, axis=0, keepdims=True)
        p = jnp.exp2(s - m)
        ms.append(m)
        ls.append(jnp.sum(p, axis=0, keepdims=True))
        ps.append(jnp.concatenate([p.astype(BF16), jnp.zeros((tm - META_KEYS, tq), BF16)], axis=0))
    accs = [_dot(meta_vt[c], ps[c]) for c in range(len(qs))]
    return tuple(ms), tuple(ls), accs


def _attend_query_tiles(q_of, k_of, vt_of, meta_k, meta_vt, write_out, nq, t, n, acc_ref, accm_ref, s_ref, st_ref):
    def matmul_scores(qi, j):
        qs, ks = q_of(qi), k_of(j)
        return [_dot_nt(ks[c], qs[c]) for c in range(n)]

    def park_scores(ss, slot, mask):
        tile_max = []
        for c in range(n):
            s = ss[c] if mask is None else jnp.where(mask, ss[c], MASK_VALUE)
            s_ref[slot, c] = s
            tile_max.append(jnp.max(s, axis=0, keepdims=True))
        return tuple(tile_max)

    h = t // 2
    diag_a = (lax.broadcasted_iota(jnp.int32, (h, t), 0)
              <= lax.broadcasted_iota(jnp.int32, (h, t), 1))
    diag_b = (lax.broadcasted_iota(jnp.int32, (h, h), 0)
              <= lax.broadcasted_iota(jnp.int32, (h, h), 1))

    def widen(c, row, later, fill):
        st_ref[c, row:row + 1, :] = jnp.full((1, t), fill, F32)
        st_ref[c, row:row + 1, h:] = later
        return st_ref[c, row:row + 1, :]

    def matmul_diag_scores(qi):
        qs, ks = q_of(qi), k_of(qi)
        return [(_dot_nt(ks[c][:h], qs[c]), _dot_nt(ks[c][h:], qs[c][h:])) for c in range(n)]

    def park_diag_scores(ss, slot):
        tile_max = []
        for c in range(n):
            sa = jnp.where(diag_a, ss[c][0], MASK_VALUE)
            sb = jnp.where(diag_b, ss[c][1], MASK_VALUE)
            s_ref[slot, c, :h] = sa
            s_ref[slot, c, h:, h:] = sb
            mb = widen(c, 0, jnp.max(sb, axis=0, keepdims=True), MASK_VALUE)
            tile_max.append(jnp.maximum(jnp.max(sa, axis=0, keepdims=True), mb))
        return tuple(tile_max)

    def update_diag(j, slot, tile_max, ms, ls):
        vts = vt_of(j)
        ms_new, ls_new = [], []
        for c in range(n):
            m_new = jnp.maximum(ms[c], tile_max[c])
            alpha = jnp.exp2(ms[c] - m_new)
            st_ref[c, 1:2, :] = m_new
            pa = jnp.exp2(s_ref[slot, c, :h] - m_new)
            pb = jnp.exp2(s_ref[slot, c, h:, h:] - st_ref[c, 1:2, h:])
            lb = widen(c, 2, jnp.sum(pb, axis=0, keepdims=True), 0.0)
            ls_new.append(alpha * ls[c] + jnp.sum(pa, axis=0, keepdims=True) + lb)
            acc_ref[c] = alpha * acc_ref[c] + _dot(vts[c][:, :h], pa.astype(BF16))
            acc_ref[c, :, h:] = acc_ref[c, :, h:] + _dot(vts[c][:, h:], pb.astype(BF16))
            ms_new.append(m_new)
        return tuple(ms_new), tuple(ls_new)

    def update(j, slot, tile_max, ms, ls):
        vts = vt_of(j)
        ms_new, ls_new = [], []
        for c in range(n):
            m_new = jnp.maximum(ms[c], tile_max[c])
            alpha = jnp.exp2(ms[c] - m_new)
            p = jnp.exp2(s_ref[slot, c] - m_new)
            ls_new.append(alpha * ls[c] + jnp.sum(p, axis=0, keepdims=True))
            acc_ref[c] = alpha * acc_ref[c] + _dot(vts[c], p.astype(BF16))
            ms_new.append(m_new)
        return tuple(ms_new), tuple(ls_new)

    def start_query_tile(qi, is_diag):
        ss = matmul_diag_scores(qi) if is_diag else matmul_scores(qi, 0)

        def rest():
            ms0, ls0, accs0 = _attend_meta(q_of(qi), meta_k, meta_vt, None)
            for c in range(n):
                accm_ref[c] = accs0[c]
            first_max = park_diag_scores(ss, 0) if is_diag else park_scores(ss, 0, None)
            return first_max, ms0, ls0

        return rest

    def query_tile(qi, first):
        first_max, ms0, ls0 = first
        acc_ref[...] = accm_ref[...]

        def step(j, slot, carry, next_is_diag):
            tile_max, ms, ls = carry
            if next_is_diag:
                nxt = park_diag_scores(matmul_diag_scores(qi), 1 - slot)
            else:
                nxt = park_scores(matmul_scores(qi, j + 1), 1 - slot, None)
            ms, ls = update(j, slot, tile_max, ms, ls)
            return nxt, ms, ls

        def pair(i, carry):
            carry = step(2 * i, 0, carry, False)
            return step(2 * i + 1, 1, carry, False)

        def last(slot, carry):
            nxt_first = start_query_tile(jnp.minimum(qi + 1, nq - 1), False)
            _, ls = update_diag(qi, slot, *carry)
            return ls, nxt_first()

        def finish_even(carry):
            return last(1, step(qi - 1, 0, carry, True))

        def finish_odd(carry):
            carry = step(qi - 2, 0, carry, False)
            return last(0, step(qi - 1, 1, carry, True))

        def finish(carry):
            return lax.cond(((qi - 1) & 1) == 0, finish_even, finish_odd, carry)

        def only_diagonal(carry):
            return last(0, carry)

        carry = (first_max, ms0, ls0)
        carry = lax.fori_loop(0, jnp.maximum(qi - 1, 0) // 2, pair, carry)
        ls, next_first = lax.cond(qi >= 1, finish, only_diagonal, carry)
        write_out(qi, [acc_ref[c] for c in range(n)], ls)
        return next_first

    lax.fori_loop(0, nq, query_tile, start_query_tile(0, True)())


def _mla_out(accs, ls):
    out_t = jnp.concatenate([accs[c] / ls[c] for c in range(len(accs))], axis=0)
    return out_t.T.astype(BF16)


def _mla_meta_operands(km_ref, vtm_ref, heads):
    meta_k = [km_ref[0, :, c * LANES:(c + 1) * LANES] for c in range(heads)]
    meta_vt = [vtm_ref[0, 0, c * MLA_V:(c + 1) * MLA_V, :] for c in range(heads)]
    return meta_k, meta_vt


def _mla_kernel(q_ref, k_ref, vt_ref, km_ref, vtm_ref, o_ref, acc_ref, accm_ref, s_ref, st_ref, *, t, nq, heads):
    lane = [slice(c * LANES, (c + 1) * LANES) for c in range(heads)]
    rows = [slice(c * MLA_V, (c + 1) * MLA_V) for c in range(heads)]

    def q_of(qi):
        off = pl.multiple_of(qi * t, t)
        return [q_ref[0, pl.ds(off, t), lane[c]] for c in range(heads)]

    def k_of(j):
        off = pl.multiple_of(j * t, t)
        return [k_ref[0, pl.ds(off, t), lane[c]] for c in range(heads)]

    def vt_of(j):
        return [vt_ref[0, j, rows[c], :] for c in range(heads)]

    def write_out(qi, accs, ls):
        o_ref[0, pl.ds(pl.multiple_of(qi * t, t), t), :] = _mla_out(accs, ls)

    meta_k, meta_vt = _mla_meta_operands(km_ref, vtm_ref, heads)
    _attend_query_tiles(q_of, k_of, vt_of, meta_k, meta_vt, write_out, nq, t, heads, acc_ref, accm_ref, s_ref, st_ref)


def _mla_meta_kernel(q_ref, km_ref, vtm_ref, o_ref, *, heads):
    tm = q_ref.shape[1]
    causal = (lax.broadcasted_iota(jnp.int32, (META_KEYS, tm), 0)
              <= lax.broadcasted_iota(jnp.int32, (META_KEYS, tm), 1))
    qs = [q_ref[0, :, c * LANES:(c + 1) * LANES] for c in range(heads)]
    meta_k, meta_vt = _mla_meta_operands(km_ref, vtm_ref, heads)
    _, ls, accs = _attend_meta(qs, meta_k, meta_vt, causal)
    o_ref[0] = _mla_out(accs, ls)


def _mla_call(q, k, vt, km, vtm, t):
    b, l, _ = q.shape
    nq = l // t
    hs = MLA_HEADS_PER_STEP
    tm = km.shape[1]
    return pl.pallas_call(
        functools.partial(_mla_kernel, t=t, nq=nq, heads=hs),
        grid=(b, MLA_HEADS // hs),
        in_specs=[
            pl.BlockSpec((1, l, hs * LANES), lambda bi, hg: (bi, 0, hg)),
            pl.BlockSpec((1, l, hs * LANES), lambda bi, hg: (bi, 0, hg)),
            pl.BlockSpec((1, nq, hs * MLA_V, t), lambda bi, hg: (bi, 0, hg, 0)),
            pl.BlockSpec((1, tm, hs * LANES), lambda bi, hg: (0, 0, hg)),
            pl.BlockSpec((1, 1, hs * MLA_V, tm), lambda bi, hg: (0, 0, hg, 0)),
        ],
        out_specs=pl.BlockSpec((1, l, hs * MLA_V), lambda bi, hg: (bi, 0, hg)),
        out_shape=jax.ShapeDtypeStruct((b, l, MLA_HEADS * MLA_V), BF16),
        scratch_shapes=[pltpu.VMEM((hs, MLA_V, t), F32), pltpu.VMEM((hs, MLA_V, t), F32),
                        pltpu.VMEM((2, hs, t, t), F32), pltpu.VMEM((hs, 8, t), F32)],
        compiler_params=_params(2),
        name="mla_attn",
    )(q, k, vt, km, vtm)


def _mla_meta_call(qm, km, vtm):
    _, tm, _ = qm.shape
    hs = MLA_HEADS_PER_STEP
    return pl.pallas_call(
        functools.partial(_mla_meta_kernel, heads=hs),
        grid=(MLA_HEADS // hs,),
        in_specs=[
            pl.BlockSpec((1, tm, hs * LANES), lambda hg: (0, 0, hg)),
            pl.BlockSpec((1, tm, hs * LANES), lambda hg: (0, 0, hg)),
            pl.BlockSpec((1, 1, hs * MLA_V, tm), lambda hg: (0, 0, hg, 0)),
        ],
        out_specs=pl.BlockSpec((1, tm, hs * MLA_V), lambda hg: (0, 0, hg)),
        out_shape=jax.ShapeDtypeStruct((1, tm, MLA_HEADS * MLA_V), BF16),
        compiler_params=_params(1),
        name="mla_meta_attn",
    )(qm, km, vtm)


def _diff_queries(q):
    lane = lax.broadcasted_iota(jnp.int32, q.shape, 1)
    zero = jnp.zeros_like(q)
    return [jnp.where(lane < DIFF_DH, q, zero), jnp.where(lane < DIFF_DH, zero, q)]


def _diff_lambda(lam_ref, lam_init):
    lp = lam_ref[...]
    return (jnp.exp(jnp.sum(lp[0:1] * lp[1:2], axis=-1, keepdims=True))
            - jnp.exp(jnp.sum(lp[2:3] * lp[3:4], axis=-1, keepdims=True)) + lam_init)


def _diff_out(accs, ls, hd, lam, g_ref, lam_init):
    od = (accs[2 * hd] / ls[2 * hd] - lam * (accs[2 * hd + 1] / ls[2 * hd + 1])).T
    return (_rms(od, g_ref[...]) * (1.0 - lam_init)).astype(BF16)


def _diff_meta_operands(km_ref, vtm_ref, heads):
    meta_k = [km_ref[0, :, (c // 2) * LANES:(c // 2 + 1) * LANES] for c in range(2 * heads)]
    meta_vt = [vtm_ref[0, 0, (c // 2) * DIFF_V:(c // 2 + 1) * DIFF_V, :] for c in range(2 * heads)]
    return meta_k, meta_vt


def _diff_kernel(q_ref, k_ref, vt_ref, km_ref, vtm_ref, lam_ref, g_ref, o_ref, acc_ref, accm_ref, s_ref, st_ref,
                 *, t, nq, heads, lam_init):
    def q_of(qi):
        off = pl.multiple_of(qi * t, t)
        qs = []
        for hd in range(heads):
            qs += _diff_queries(q_ref[0, pl.ds(off, t), hd * LANES:(hd + 1) * LANES])
        return qs

    def k_of(j):
        off = pl.multiple_of(j * t, t)
        ks = [k_ref[0, pl.ds(off, t), hd * LANES:(hd + 1) * LANES] for hd in range(heads)]
        return [ks[c // 2] for c in range(2 * heads)]

    def vt_of(j):
        vts = [vt_ref[0, j, hd * DIFF_V:(hd + 1) * DIFF_V, :] for hd in range(heads)]
        return [vts[c // 2] for c in range(2 * heads)]

    lam = _diff_lambda(lam_ref, lam_init)

    def write_out(qi, accs, ls):
        off = pl.multiple_of(qi * t, t)
        for hd in range(heads):
            o_ref[0, pl.ds(off, t), hd * DIFF_V:(hd + 1) * DIFF_V] = _diff_out(
                accs, ls, hd, lam, g_ref, lam_init)

    meta_k, meta_vt = _diff_meta_operands(km_ref, vtm_ref, heads)
    _attend_query_tiles(q_of, k_of, vt_of, meta_k, meta_vt, write_out, nq, t, 2 * heads, acc_ref, accm_ref, s_ref, st_ref)


def _diff_meta_kernel(q_ref, km_ref, vtm_ref, lam_ref, g_ref, o_ref, *, heads, lam_init):
    tm = q_ref.shape[1]
    causal = (lax.broadcasted_iota(jnp.int32, (META_KEYS, tm), 0)
              <= lax.broadcasted_iota(jnp.int32, (META_KEYS, tm), 1))
    qs = []
    for hd in range(heads):
        qs += _diff_queries(q_ref[0, :, hd * LANES:(hd + 1) * LANES])
    meta_k, meta_vt = _diff_meta_operands(km_ref, vtm_ref, heads)
    _, ls, accs = _attend_meta(qs, meta_k, meta_vt, causal)
    lam = _diff_lambda(lam_ref, lam_init)
    for hd in range(heads):
        o_ref[0, :, hd * DIFF_V:(hd + 1) * DIFF_V] = _diff_out(accs, ls, hd, lam, g_ref, lam_init)


def _diff_call(q, k, vt, km, vtm, lam_p, g_sub, t, lam_init):
    b, l, _ = q.shape
    nq = l // t
    hs = DIFF_HEADS_PER_STEP
    tm = km.shape[1]
    return pl.pallas_call(
        functools.partial(_diff_kernel, t=t, nq=nq, heads=hs, lam_init=lam_init),
        grid=(b, DIFF_HEADS // hs),
        in_specs=[
            pl.BlockSpec((1, l, hs * LANES), lambda bi, hg: (bi, 0, hg)),
            pl.BlockSpec((1, l, hs * LANES), lambda bi, hg: (bi, 0, hg)),
            pl.BlockSpec((1, nq, hs * DIFF_V, t), lambda bi, hg: (bi, 0, hg, 0)),
            pl.BlockSpec((1, tm, hs * LANES), lambda bi, hg: (0, 0, hg)),
            pl.BlockSpec((1, 1, hs * DIFF_V, tm), lambda bi, hg: (0, 0, hg, 0)),
            _const_spec(lam_p.shape), _const_spec(g_sub.shape),
        ],
        out_specs=pl.BlockSpec((1, l, hs * DIFF_V), lambda bi, hg: (bi, 0, hg)),
        out_shape=jax.ShapeDtypeStruct((b, l, DIFF_W), BF16),
        scratch_shapes=[pltpu.VMEM((2 * hs, DIFF_V, t), F32), pltpu.VMEM((2 * hs, DIFF_V, t), F32),
                        pltpu.VMEM((2, 2 * hs, t, t), F32), pltpu.VMEM((2 * hs, 8, t), F32)],
        compiler_params=_params(2),
        name="diff_attn",
    )(q, k, vt, km, vtm, lam_p, g_sub)


def _diff_meta_call(qm, km, vtm, lam_p, g_sub, lam_init):
    _, tm, _ = qm.shape
    hs = DIFF_HEADS_PER_STEP
    return pl.pallas_call(
        functools.partial(_diff_meta_kernel, heads=hs, lam_init=lam_init),
        grid=(DIFF_HEADS // hs,),
        in_specs=[
            pl.BlockSpec((1, tm, hs * LANES), lambda hg: (0, 0, hg)),
            pl.BlockSpec((1, tm, hs * LANES), lambda hg: (0, 0, hg)),
            pl.BlockSpec((1, 1, hs * DIFF_V, tm), lambda hg: (0, 0, hg, 0)),
            _const_spec(lam_p.shape), _const_spec(g_sub.shape),
        ],
        out_specs=pl.BlockSpec((1, tm, hs * DIFF_V), lambda hg: (0, 0, hg)),
        out_shape=jax.ShapeDtypeStruct((1, tm, DIFF_W), BF16),
        compiler_params=_params(1),
        name="diff_meta_attn",
    )(qm, km, vtm, lam_p, g_sub)


def _merge_kernel(r_ref, oa_ref, od_ref, g_pre_ref, wg_ref, woa_ref, wod_ref, wout_ref, g_post_ref, o_ref):
    r = r_ref[...]
    d = r.shape[-1]
    h = _rms(r, g_pre_ref[...]).astype(BF16)
    g = _dot(h, wg_ref[...])
    a = _dot(oa_ref[...], woa_ref[...])
    bb = _dot(od_ref[...], wod_ref[...])
    mixed = (jax.nn.sigmoid(g[:, :d]) * a + jax.nn.sigmoid(g[:, d:]) * bb).astype(BF16)
    mix = _dot(mixed, wout_ref[...])
    o_ref[...] = r + _rms(mix, g_post_ref[...])


def _merge_call(r2, oa2, od2, g_pre, wg, woa, wod, wout, g_post, tm):
    n, d = r2.shape
    row = lambda w: pl.BlockSpec((tm, w), lambda i: (i, 0))
    return pl.pallas_call(
        _merge_kernel,
        grid=(n // tm,),
        in_specs=[row(d), row(oa2.shape[1]), row(od2.shape[1]),
                  _const_spec(g_pre.shape), _const_spec(wg.shape), _const_spec(woa.shape),
                  _const_spec(wod.shape), _const_spec(wout.shape), _const_spec(g_post.shape)],
        out_specs=row(d),
        out_shape=jax.ShapeDtypeStruct((n, d), F32),
        compiler_params=_params(1),
        name="merge",
    )(r2, oa2, od2, g_pre, wg, woa, wod, wout, g_post)


def _ffn_kernel(r_ref, g_pre_ref, wgu_ref, wd_ref, g_post_ref, o_ref):
    r = r_ref[...]
    dff = wd_ref.shape[0]
    h = _rms(r, g_pre_ref[...]).astype(BF16)
    gu = _dot(h, wgu_ref[...])
    gate = gu[:, :dff]
    act = (gate * jax.nn.sigmoid(gate) * gu[:, dff:]).astype(BF16)
    ff = _dot(act, wd_ref[...])
    o_ref[...] = r + _rms(ff, g_post_ref[...])


def _ffn_call(r2, g_pre, wgu, wd, g_post, tm):
    n, d = r2.shape
    row = pl.BlockSpec((tm, d), lambda i: (i, 0))
    return pl.pallas_call(
        _ffn_kernel,
        grid=(n // tm,),
        in_specs=[row, _const_spec(g_pre.shape), _const_spec(wgu.shape),
                  _const_spec(wd.shape), _const_spec(g_post.shape)],
        out_specs=row,
        out_shape=jax.ShapeDtypeStruct((n, d), F32),
        compiler_params=_params(1),
        name="ffn",
    )(r2, g_pre, wgu, wd, g_post)


def _rope_tables(pos, rot_dim, group_starts):
    half = rot_dim // 2
    lane = jnp.arange(LANES)
    first = jnp.zeros((LANES,), bool)
    second = jnp.zeros((LANES,), bool)
    for g in group_starts:
        first = first | ((lane >= g) & (lane < g + half))
        second = second | ((lane >= g + half) & (lane < g + rot_dim))
    freq = jnp.where(first | second, (lane % half).astype(F32), 0.0)
    inv = ROPE_THETA ** (-(2.0 * freq) / rot_dim)
    ang = pos.astype(F32)[:, None] * inv[None, :]
    cos, sin = jnp.cos(ang), jnp.sin(ang)
    c = jnp.where(first | second, cos, 1.0)
    sa = jnp.where(first, -sin, 0.0)
    sb = jnp.where(second, sin, 0.0)
    return jnp.stack([c, sa, sb])


def _pick_tile(n, candidates):
    for c in candidates:
        if n % c == 0:
            return c
    raise ValueError(f"no tile in {candidates} divides {n}")


def kernel(x, meta_tokens, norm_mix_pre, norm_mix_post, norm_ffn_pre, norm_ffn_post, w_in, mla_q_norm, mla_kv_norm, mla_w_uq, mla_w_ukv, w_o_mla, diff_lambda, diff_subln, w_o_diff, w_out, w_gate_up, w_down):
    dt = x.dtype
    b, s, d = x.shape
    depth = w_in.shape[0]
    r_x = x
    r_m = jnp.zeros((1, META_BLOCK, d), dt).at[0, :N_META].set(meta_tokens.astype(dt))

    t_seq = _pick_tile(s, (512, 256, 128))
    t_tok = _pick_tile(b * s, (512, 256, 128))

    mla_scale = LOG2E / math.sqrt(MLA_NOPE + MLA_ROPE)
    diff_scale = LOG2E / math.sqrt(DIFF_DH)

    def tables(pos):
        return (_rope_tables(pos, MLA_ROPE, (MLA_NOPE,)), _rope_tables(pos, DIFF_ROPE, (0, DIFF_DH)))

    tabs_x = tables(N_META + jnp.arange(s))
    tabs_m = tables(jnp.arange(META_BLOCK))

    c0 = MLA_Q_RANK
    c1 = c0 + MLA_KV_RANK
    c2 = c1 + MLA_ROPE
    c_dv = c2 + 2 * DIFF_W
    c3 = c_dv + DIFF_W
    row1 = lambda v: v.reshape(1, -1).astype(F32)

    for layer in range(depth):
        wl = w_in[layer]
        kpe_w = jnp.zeros((d, LANES), dt).at[:, MLA_NOPE:MLA_NOPE + MLA_ROPE].set(wl[:, c1:c2])
        w1 = jnp.concatenate([wl[:, :c1], kpe_w, wl[:, c2:c_dv]], axis=1).astype(BF16)
        wdvt = wl[:, c_dv:c3].T.astype(BF16)
        wg = wl[:, c3:].astype(BF16)
        wuq = jnp.pad(mla_w_uq[layer].reshape(MLA_Q_RANK, MLA_HEADS, MLA_NOPE + MLA_ROPE),
                      ((0, 0), (0, 0), (0, LANES - MLA_NOPE - MLA_ROPE))
                      ).reshape(MLA_Q_RANK, MLA_W).astype(BF16)
        wukv = mla_w_ukv[layer].reshape(MLA_KV_RANK, MLA_HEADS, MLA_NOPE + MLA_V)
        wk = jnp.pad(wukv[:, :, :MLA_NOPE], ((0, 0), (0, 0), (0, LANES - MLA_NOPE))
                     ).reshape(MLA_KV_RANK, MLA_W).astype(BF16)
        wvt = wukv[:, :, MLA_NOPE:].reshape(MLA_KV_RANK, MLA_HEADS * MLA_V).T.astype(BF16)
        proj_w = (row1(norm_mix_pre[layer]), w1, wdvt, row1(mla_q_norm[layer]), wuq,
                  row1(mla_kv_norm[layer]), wk, wvt)
        merge_w = (row1(norm_mix_pre[layer]), wg, w_o_mla[layer].astype(BF16),
                   w_o_diff[layer].astype(BF16), w_out[layer].astype(BF16), row1(norm_mix_post[layer]))
        ffn_w = (row1(norm_ffn_pre[layer]), w_gate_up[layer].astype(BF16),
                 w_down[layer].astype(BF16), row1(norm_ffn_post[layer]))
        lam_p, g_sub = diff_lambda[layer].astype(F32), row1(diff_subln[layer])
        lam_init = 0.8 - 0.6 * math.exp(-0.3 * layer)

        q, k, vt, dq, dk, dvt = _proj_call(r_x, *proj_w, *tabs_x, t_seq, mla_scale, diff_scale)
        qm, km, vtm, dqm, dkm, dvtm = _proj_call(r_m, *proj_w, *tabs_m, META_BLOCK, mla_scale, diff_scale)

        out_a = _mla_call(q, k, vt, km, vtm, t_seq)
        out_d = _diff_call(dq, dk, dvt, dkm, dvtm, lam_p, g_sub, t_seq, lam_init)
        r2 = _merge_call(r_x.reshape(b * s, d), out_a.reshape(b * s, -1), out_d.reshape(b * s, -1),
                         *merge_w, t_tok)
        r_x = _ffn_call(r2, *ffn_w, t_tok).reshape(b, s, d)

        if layer < depth - 1:
            out_a_m = _mla_meta_call(qm, km, vtm)
            out_d_m = _diff_meta_call(dqm, dkm, dvtm, lam_p, g_sub, lam_init)
            r2 = _merge_call(r_m[0], out_a_m[0], out_d_m[0], *merge_w, META_BLOCK)
            r_m = _ffn_call(r2, *ffn_w, META_BLOCK)[None]

    return r_x
```

```python
import functools
import math

import jax
import jax.numpy as jnp
from jax import lax
from jax.experimental import pallas as pl
from jax.experimental.pallas import tpu as pltpu

N_META = 16
META_BLOCK = 128
META_KEYS = 16
ROPE_THETA = 500000.0
RMS_EPS = 1e-6

MLA_HEADS = 8
MLA_NOPE = 64
MLA_ROPE = 32
MLA_V = 64
MLA_Q_RANK = 384
MLA_KV_RANK = 256

DIFF_HEADS = 4
DIFF_DH = 64
DIFF_ROPE = DIFF_DH // 4
DIFF_V = 2 * DIFF_DH

LANES = 128
MLA_W = MLA_HEADS * LANES
DIFF_W = DIFF_HEADS * DIFF_V
MLA_HEADS_PER_STEP = 4
PROJ_ROW_GROUPS = 2
DIFF_HEADS_PER_STEP = 2
VMEM_LIMIT_BYTES = 56 * 1024 * 1024
MASK_VALUE = -1e30
LOG2E = 1.4426950408889634

BF16 = jnp.bfloat16
F32 = jnp.float32


def _rms(x, gain):
    ms = jnp.mean(x * x, axis=-1, keepdims=True)
    return x * lax.rsqrt(ms + RMS_EPS) * gain


def _rope(x, c, sa, sb, half):
    return x * c + pltpu.roll(x, LANES - half, 1) * sa + pltpu.roll(x, half, 1) * sb


def _dot(a, b):
    return jnp.dot(a, b, preferred_element_type=F32)


def _dot_nt(a, b):
    return lax.dot_general(a, b, (((1,), (1,)), ((), ())), preferred_element_type=F32)


def _const_spec(shape):
    nd = len(shape)
    return pl.BlockSpec(shape, lambda *_: (0,) * nd, pipeline_mode=pl.Buffered(1))


def _params(n_grid):
    return pltpu.CompilerParams(
        dimension_semantics=("arbitrary",) * n_grid,
        vmem_limit_bytes=VMEM_LIMIT_BYTES,
    )


def _proj_kernel(r_ref, g_pre_ref, w1_ref, wdvt_ref, g_q_ref, wuq_ref, g_kv_ref, wk_ref, wvt_ref,
                 tk_ref, td_ref,
                 q_ref, k_ref, vt_ref, dq_ref, dk_ref, dvt_ref, *, mla_scale, diff_scale):
    tm = r_ref.shape[1]
    ng = PROJ_ROW_GROUPS if tm % (PROJ_ROW_GROUPS * LANES) == 0 else 1
    gr = tm // ng
    for gi in range(ng):
        rs = slice(gi * gr, (gi + 1) * gr)
        r = r_ref[0, rs, :]
        h = _rms(r, g_pre_ref[...]).astype(BF16)
        p1 = _dot(h, w1_ref[...])
        o = 0
        q_lat = p1[:, o:o + MLA_Q_RANK]; o += MLA_Q_RANK
        kv_lat = p1[:, o:o + MLA_KV_RANK]; o += MLA_KV_RANK
        kpe = p1[:, o:o + LANES]; o += LANES
        dq = p1[:, o:o + DIFF_W]; o += DIFF_W
        dk = p1[:, o:o + DIFF_W]

        cq = _rms(q_lat, g_q_ref[...]).astype(BF16)
        qa = _dot(cq, wuq_ref[...])
        ct, sat, sbt = tk_ref[0, rs, :], tk_ref[1, rs, :], tk_ref[2, rs, :]
        for hd in range(MLA_HEADS):
            sl = slice(hd * LANES, (hd + 1) * LANES)
            q_ref[0, rs, sl] = (_rope(qa[:, sl], ct, sat, sbt, MLA_ROPE // 2) * mla_scale).astype(BF16)

        ckv = _rms(kv_lat, g_kv_ref[...]).astype(BF16)
        kn = _dot(ckv, wk_ref[...])
        kpe_r = _rope(kpe, ct, sat, sbt, MLA_ROPE // 2)
        for hd in range(MLA_HEADS):
            sl = slice(hd * LANES, (hd + 1) * LANES)
            k_ref[0, rs, sl] = (kn[:, sl] + kpe_r).astype(BF16)
        vt_ref[0, 0, :, rs] = _dot_nt(wvt_ref[...], ckv).astype(BF16)

        for hd in range(DIFF_HEADS):
            sl = slice(hd * LANES, (hd + 1) * LANES)
            cd, sad, sbd = td_ref[0, rs, :], td_ref[1, rs, :], td_ref[2, rs, :]
            dq_ref[0, rs, sl] = (_rope(dq[:, sl], cd, sad, sbd, DIFF_ROPE // 2) * diff_scale).astype(BF16)
            dk_ref[0, rs, sl] = _rope(dk[:, sl], cd, sad, sbd, DIFF_ROPE // 2).astype(BF16)
        dvt_ref[0, 0, :, rs] = _dot_nt(wdvt_ref[...], h).astype(BF16)


def _proj_call(r, g_pre, w1, wdvt, g_q, wuq, g_kv, wk, wvt, tk, td, tm, mla_scale, diff_scale):
    b, l, d = r.shape
    nt = l // tm
    tab_spec = pl.BlockSpec((3, tm, LANES), lambda ti, bi: (0, ti, 0))

    def rows(n):
        return pl.BlockSpec((1, tm, n), lambda ti, bi: (bi, ti, 0))

    def cols(n):
        return pl.BlockSpec((1, 1, n, tm), lambda ti, bi: (bi, ti, 0, 0))

    nv, ndv = wvt.shape[0], wdvt.shape[0]
    return pl.pallas_call(
        functools.partial(_proj_kernel, mla_scale=mla_scale, diff_scale=diff_scale),
        grid=(nt, b),
        in_specs=[
            rows(d),
            _const_spec(g_pre.shape), _const_spec(w1.shape), _const_spec(wdvt.shape),
            _const_spec(g_q.shape), _const_spec(wuq.shape),
            _const_spec(g_kv.shape), _const_spec(wk.shape), _const_spec(wvt.shape),
            tab_spec, tab_spec,
        ],
        out_specs=[rows(MLA_W), rows(MLA_W), cols(nv), rows(DIFF_W), rows(DIFF_W), cols(ndv)],
        out_shape=[
            jax.ShapeDtypeStruct((b, l, MLA_W), BF16),
            jax.ShapeDtypeStruct((b, l, MLA_W), BF16),
            jax.ShapeDtypeStruct((b, nt, nv, tm), BF16),
            jax.ShapeDtypeStruct((b, l, DIFF_W), BF16),
            jax.ShapeDtypeStruct((b, l, DIFF_W), BF16),
            jax.ShapeDtypeStruct((b, nt, ndv, tm), BF16),
        ],
        compiler_params=_params(2),
        name="proj",
    )(r, g_pre, w1, wdvt, g_q, wuq, g_kv, wk, wvt, tk, td)


def _attend_meta(qs, meta_k, meta_vt, causal):
    tm, tq = meta_k[0].shape[0], qs[0].shape[0]
    valid = lax.broadcasted_iota(jnp.int32, (META_KEYS, tq), 0) < N_META
    if causal is not None:
        valid = jnp.logical_and(valid, causal)
    ss = [_dot_nt(meta_k[c][:META_KEYS], qs[c]) for c in range(len(qs))]
    ms, ls, ps = [], [], []
    for s in ss:
        if causal is not None or META_KEYS != N_META:
            s = jnp.where(valid, s, MASK_VALUE)
        m = jnp.max(s, axis=0, keepdims=True)
        p = jnp.exp2(s - m)
        ms.append(m)
        ls.append(jnp.sum(p, axis=0, keepdims=True))
        ps.append(jnp.concatenate([p.astype(BF16), jnp.zeros((tm - META_KEYS, tq), BF16)], axis=0))
    accs = [_dot(meta_vt[c], ps[c]) for c in range(len(qs))]
    return tuple(ms), tuple(ls), accs


def _attend_query_tiles(q_of, k_of, vt_of, meta_k, meta_vt, write_out, nq, t, n, acc_ref, accm_ref, s_ref, st_ref):
    def matmul_scores(qi, j):
        qs, ks = q_of(qi), k_of(j)
        return [_dot_nt(ks[c], qs[c]) for c in range(n)]

    def park_scores(ss, slot, mask):
        tile_max = []
        for c in range(n):
            s = ss[c] if mask is None else jnp.where(mask, ss[c], MASK_VALUE)
            s_ref[slot, c] = s
            tile_max.append(jnp.max(s, axis=0, keepdims=True))
        return tuple(tile_max)

    h = t // 2
    diag_a = (lax.broadcasted_iota(jnp.int32, (h, t), 0)
              <= lax.broadcasted_iota(jnp.int32, (h, t), 1))
    diag_b = (lax.broadcasted_iota(jnp.int32, (h, h), 0)
              <= lax.broadcasted_iota(jnp.int32, (h, h), 1))

    def widen(c, row, later, fill):
        st_ref[c, row:row + 1, :] = jnp.full((1, t), fill, F32)
        st_ref[c, row:row + 1, h:] = later
        return st_ref[c, row:row + 1, :]

    def matmul_diag_scores(qi):
        qs, ks = q_of(qi), k_of(qi)
        return [(_dot_nt(ks[c][:h], qs[c]), _dot_nt(ks[c][h:], qs[c][h:])) for c in range(n)]

    def park_diag_scores(ss, slot):
        tile_max = []
        for c in range(n):
            sa = jnp.where(diag_a, ss[c][0], MASK_VALUE)
            sb = jnp.where(diag_b, ss[c][1], MASK_VALUE)
            s_ref[slot, c, :h] = sa
            s_ref[slot, c, h:, h:] = sb
            mb = widen(c, 0, jnp.max(sb, axis=0, keepdims=True), MASK_VALUE)
            tile_max.append(jnp.maximum(jnp.max(sa, axis=0, keepdims=True), mb))
        return tuple(tile_max)

    def update_diag(j, slot, tile_max, ms, ls):
        vts = vt_of(j)
        ms_new, ls_new = [], []
        for c in range(n):
            m_new = jnp.maximum(ms[c], tile_max[c])
            alpha = jnp.exp2(ms[c] - m_new)
            st_ref[c, 1:2, :] = m_new
            pa = jnp.exp2(s_ref[slot, c, :h] - m_new)
            pb = jnp.exp2(s_ref[slot, c, h:, h:] - st_ref[c, 1:2, h:])
            lb = widen(c, 2, jnp.sum(pb, axis=0, keepdims=True), 0.0)
            ls_new.append(alpha * ls[c] + jnp.sum(pa, axis=0, keepdims=True) + lb)
            acc_ref[c] = alpha * acc_ref[c] + _dot(vts[c][:, :h], pa.astype(BF16))
            acc_ref[c, :, h:] = acc_ref[c, :, h:] + _dot(vts[c][:, h:], pb.astype(BF16))
            ms_new.append(m_new)
        return tuple(ms_new), tuple(ls_new)

    def update(j, slot, tile_max, ms, ls):
        vts = vt_of(j)
        ms_new, ls_new = [], []
        for c in range(n):
            m_new = jnp.maximum(ms[c], tile_max[c])
            alpha = jnp.exp2(ms[c] - m_new)
            p = jnp.exp2(s_ref[slot, c] - m_new)
            ls_new.append(alpha * ls[c] + jnp.sum(p, axis=0, keepdims=True))
            acc_ref[c] = alpha * acc_ref[c] + _dot(vts[c], p.astype(BF16))
            ms_new.append(m_new)
        return tuple(ms_new), tuple(ls_new)

    def start_query_tile(qi, is_diag):
        ss = matmul_diag_scores(qi) if is_diag else matmul_scores(qi, 0)

        def rest():
            ms0, ls0, accs0 = _attend_meta(q_of(qi), meta_k, meta_vt, None)
            for c in range(n):
                accm_ref[c] = accs0[c]
            first_max = park_diag_scores(ss, 0) if is_diag else park_scores(ss, 0, None)
            return first_max, ms0, ls0

        return rest

    def query_tile(qi, first):
        first_max, ms0, ls0 = first
        acc_ref[...] = accm_ref[...]

        def step(j, slot, carry, next_is_diag):
            tile_max, ms, ls = carry
            if next_is_diag:
                nxt = park_diag_scores(matmul_diag_scores(qi), 1 - slot)
            else:
                nxt = park_scores(matmul_scores(qi, j + 1), 1 - slot, None)
            ms, ls = update(j, slot, tile_max, ms, ls)
            return nxt, ms, ls

        def pair(i, carry):
            carry = step(2 * i, 0, carry, False)
            return step(2 * i + 1, 1, carry, False)

        def last(slot, carry):
            nxt_first = start_query_tile(jnp.minimum(qi + 1, nq - 1), False)
            _, ls = update_diag(qi, slot, *carry)
            return ls, nxt_first()

        def finish_even(carry):
            return last(1, step(qi - 1, 0, carry, True))

        def finish_odd(carry):
            carry = step(qi - 2, 0, carry, False)
            return last(0, step(qi - 1, 1, carry, True))

        def finish(carry):
            return lax.cond(((qi - 1) & 1) == 0, finish_even, finish_odd, carry)

        def only_diagonal(carry):
            return last(0, carry)

        carry = (first_max, ms0, ls0)
        carry = lax.fori_loop(0, jnp.maximum(qi - 1, 0) // 2, pair, carry)
        ls, next_first = lax.cond(qi >= 1, finish, only_diagonal, carry)
        write_out(qi, [acc_ref[c] for c in range(n)], ls)
        return next_first

    lax.fori_loop(0, nq, query_tile, start_query_tile(0, True)())


def _mla_out(accs, ls):
    out_t = jnp.concatenate([accs[c] / ls[c] for c in range(len(accs))], axis=0)
    return out_t.T.astype(BF16)


def _mla_meta_operands(km_ref, vtm_ref, heads):
    meta_k = [km_ref[0, :, c * LANES:(c + 1) * LANES] for c in range(heads)]
    meta_vt = [vtm_ref[0, 0, c * MLA_V:(c + 1) * MLA_V, :] for c in range(heads)]
    return meta_k, meta_vt


def _mla_kernel(q_ref, k_ref, vt_ref, km_ref, vtm_ref, o_ref, acc_ref, accm_ref, s_ref, st_ref, *, t, nq, heads):
    lane = [slice(c * LANES, (c + 1) * LANES) for c in range(heads)]
    rows = [slice(c * MLA_V, (c + 1) * MLA_V) for c in range(heads)]

    def q_of(qi):
        off = pl.multiple_of(qi * t, t)
        return [q_ref[0, pl.ds(off, t), lane[c]] for c in range(heads)]

    def k_of(j):
        off = pl.multiple_of(j * t, t)
        return [k_ref[0, pl.ds(off, t), lane[c]] for c in range(heads)]

    def vt_of(j):
        return [vt_ref[0, j, rows[c], :] for c in range(heads)]

    def write_out(qi, accs, ls):
        o_ref[0, pl.ds(pl.multiple_of(qi * t, t), t), :] = _mla_out(accs, ls)

    meta_k, meta_vt = _mla_meta_operands(km_ref, vtm_ref, heads)
    _attend_query_tiles(q_of, k_of, vt_of, meta_k, meta_vt, write_out, nq, t, heads, acc_ref, accm_ref, s_ref, st_ref)


def _mla_meta_kernel(q_ref, km_ref, vtm_ref, o_ref, *, heads):
    tm = q_ref.shape[1]
    causal = (lax.broadcasted_iota(jnp.int32, (META_KEYS, tm), 0)
              <= lax.broadcasted_iota(jnp.int32, (META_KEYS, tm), 1))
    qs = [q_ref[0, :, c * LANES:(c + 1) * LANES] for c in range(heads)]
    meta_k, meta_vt = _mla_meta_operands(km_ref, vtm_ref, heads)
    _, ls, accs = _attend_meta(qs, meta_k, meta_vt, causal)
    o_ref[0] = _mla_out(accs, ls)


def _mla_call(q, k, vt, km, vtm, t):
    b, l, _ = q.shape
    nq = l // t
    hs = MLA_HEADS_PER_STEP
    tm = km.shape[1]
    return pl.pallas_call(
        functools.partial(_mla_kernel, t=t, nq=nq, heads=hs),
        grid=(b, MLA_HEADS // hs),
        in_specs=[
            pl.BlockSpec((1, l, hs * LANES), lambda bi, hg: (bi, 0, hg)),
            pl.BlockSpec((1, l, hs * LANES), lambda bi, hg: (bi, 0, hg)),
            pl.BlockSpec((1, nq, hs * MLA_V, t), lambda bi, hg: (bi, 0, hg, 0)),
            pl.BlockSpec((1, tm, hs * LANES), lambda bi, hg: (0, 0, hg)),
            pl.BlockSpec((1, 1, hs * MLA_V, tm), lambda bi, hg: (0, 0, hg, 0)),
        ],
        out_specs=pl.BlockSpec((1, l, hs * MLA_V), lambda bi, hg: (bi, 0, hg)),
        out_shape=jax.ShapeDtypeStruct((b, l, MLA_HEADS * MLA_V), BF16),
        scratch_shapes=[pltpu.VMEM((hs, MLA_V, t), F32), pltpu.VMEM((hs, MLA_V, t), F32),
                        pltpu.VMEM((2, hs, t, t), F32), pltpu.VMEM((hs, 8, t), F32)],
        compiler_params=_params(2),
        name="mla_attn",
    )(q, k, vt, km, vtm)


def _mla_meta_call(qm, km, vtm):
    _, tm, _ = qm.shape
    hs = MLA_HEADS_PER_STEP
    return pl.pallas_call(
        functools.partial(_mla_meta_kernel, heads=hs),
        grid=(MLA_HEADS // hs,),
        in_specs=[
            pl.BlockSpec((1, tm, hs * LANES), lambda hg: (0, 0, hg)),
            pl.BlockSpec((1, tm, hs * LANES), lambda hg: (0, 0, hg)),
            pl.BlockSpec((1, 1, hs * MLA_V, tm), lambda hg: (0, 0, hg, 0)),
        ],
        out_specs=pl.BlockSpec((1, tm, hs * MLA_V), lambda hg: (0, 0, hg)),
        out_shape=jax.ShapeDtypeStruct((1, tm, MLA_HEADS * MLA_V), BF16),
        compiler_params=_params(1),
        name="mla_meta_attn",
    )(qm, km, vtm)


def _diff_queries(q):
    lane = lax.broadcasted_iota(jnp.int32, q.shape, 1)
    zero = jnp.zeros_like(q)
    return [jnp.where(lane < DIFF_DH, q, zero), jnp.where(lane < DIFF_DH, zero, q)]


def _diff_lambda(lam_ref, lam_init):
    lp = lam_ref[...]
    return (jnp.exp(jnp.sum(lp[0:1] * lp[1:2], axis=-1, keepdims=True))
            - jnp.exp(jnp.sum(lp[2:3] * lp[3:4], axis=-1, keepdims=True)) + lam_init)


def _diff_out(accs, ls, hd, lam, g_ref, lam_init):
    od = (accs[2 * hd] / ls[2 * hd] - lam * (accs[2 * hd + 1] / ls[2 * hd + 1])).T
    return (_rms(od, g_ref[...]) * (1.0 - lam_init)).astype(BF16)


def _diff_meta_operands(km_ref, vtm_ref, heads):
    meta_k = [km_ref[0, :, (c // 2) * LANES:(c // 2 + 1) * LANES] for c in range(2 * heads)]
    meta_vt = [vtm_ref[0, 0, (c // 2) * DIFF_V:(c // 2 + 1) * DIFF_V, :] for c in range(2 * heads)]
    return meta_k, meta_vt


def _diff_kernel(q_ref, k_ref, vt_ref, km_ref, vtm_ref, lam_ref, g_ref, o_ref, acc_ref, accm_ref, s_ref, st_ref,
                 *, t, nq, heads, lam_init):
    def q_of(qi):
        off = pl.multiple_of(qi * t, t)
        qs = []
        for hd in range(heads):
            qs += _diff_queries(q_ref[0, pl.ds(off, t), hd * LANES:(hd + 1) * LANES])
        return qs

    def k_of(j):
        off = pl.multiple_of(j * t, t)
        ks = [k_ref[0, pl.ds(off, t), hd * LANES:(hd + 1) * LANES] for hd in range(heads)]
        return [ks[c // 2] for c in range(2 * heads)]

    def vt_of(j):
        vts = [vt_ref[0, j, hd * DIFF_V:(hd + 1) * DIFF_V, :] for hd in range(heads)]
        return [vts[c // 2] for c in range(2 * heads)]

    lam = _diff_lambda(lam_ref, lam_init)

    def write_out(qi, accs, ls):
        off = pl.multiple_of(qi * t, t)
        for hd in range(heads):
            o_ref[0, pl.ds(off, t), hd * DIFF_V:(hd + 1) * DIFF_V] = _diff_out(
                accs, ls, hd, lam, g_ref, lam_init)

    meta_k, meta_vt = _diff_meta_operands(km_ref, vtm_ref, heads)
    _attend_query_tiles(q_of, k_of, vt_of, meta_k, meta_vt, write_out, nq, t, 2 * heads, acc_ref, accm_ref, s_ref, st_ref)


def _diff_meta_kernel(q_ref, km_ref, vtm_ref, lam_ref, g_ref, o_ref, *, heads, lam_init):
    tm = q_ref.shape[1]
    causal = (lax.broadcasted_iota(jnp.int32, (META_KEYS, tm), 0)
              <= lax.broadcasted_iota(jnp.int32, (META_KEYS, tm), 1))
    qs = []
    for hd in range(heads):
        qs += _diff_queries(q_ref[0, :, hd * LANES:(hd + 1) * LANES])
    meta_k, meta_vt = _diff_meta_operands(km_ref, vtm_ref, heads)
    _, ls, accs = _attend_meta(qs, meta_k, meta_vt, causal)
    lam = _diff_lambda(lam_ref, lam_init)
    for hd in range(heads):
        o_ref[0, :, hd * DIFF_V:(hd + 1) * DIFF_V] = _diff_out(accs, ls, hd, lam, g_ref, lam_init)


def _diff_call(q, k, vt, km, vtm, lam_p, g_sub, t, lam_init):
    b, l, _ = q.shape
    nq = l // t
    hs = DIFF_HEADS_PER_STEP
    tm = km.shape[1]
    return pl.pallas_call(
        functools.partial(_diff_kernel, t=t, nq=nq, heads=hs, lam_init=lam_init),
        grid=(b, DIFF_HEADS // hs),
        in_specs=[
            pl.BlockSpec((1, l, hs * LANES), lambda bi, hg: (bi, 0, hg)),
            pl.BlockSpec((1, l, hs * LANES), lambda bi, hg: (bi, 0, hg)),
            pl.BlockSpec((1, nq, hs * DIFF_V, t), lambda bi, hg: (bi, 0, hg, 0)),
            pl.BlockSpec((1, tm, hs * LANES), lambda bi, hg: (0, 0, hg)),
            pl.BlockSpec((1, 1, hs * DIFF_V, tm), lambda bi, hg: (0, 0, hg, 0)),
            _const_spec(lam_p.shape), _const_spec(g_sub.shape),
        ],
        out_specs=pl.BlockSpec((1, l, hs * DIFF_V), lambda bi, hg: (bi, 0, hg)),
        out_shape=jax.ShapeDtypeStruct((b, l, DIFF_W), BF16),
        scratch_shapes=[pltpu.VMEM((2 * hs, DIFF_V, t), F32), pltpu.VMEM((2 * hs, DIFF_V, t), F32),
                        pltpu.VMEM((2, 2 * hs, t, t), F32), pltpu.VMEM((2 * hs, 8, t), F32)],
        compiler_params=_params(2),
        name="diff_attn",
    )(q, k, vt, km, vtm, lam_p, g_sub)


def _diff_meta_call(qm, km, vtm, lam_p, g_sub, lam_init):
    _, tm, _ = qm.shape
    hs = DIFF_HEADS_PER_STEP
    return pl.pallas_call(
        functools.partial(_diff_meta_kernel, heads=hs, lam_init=lam_init),
        grid=(DIFF_HEADS // hs,),
        in_specs=[
            pl.BlockSpec((1, tm, hs * LANES), lambda hg: (0, 0, hg)),
            pl.BlockSpec((1, tm, hs * LANES), lambda hg: (0, 0, hg)),
            pl.BlockSpec((1, 1, hs * DIFF_V, tm), lambda hg: (0, 0, hg, 0)),
            _const_spec(lam_p.shape), _const_spec(g_sub.shape),
        ],
        out_specs=pl.BlockSpec((1, tm, hs * DIFF_V), lambda hg: (0, 0, hg)),
        out_shape=jax.ShapeDtypeStruct((1, tm, DIFF_W), BF16),
        compiler_params=_params(1),
        name="diff_meta_attn",
    )(qm, km, vtm, lam_p, g_sub)


def _merge_kernel(r_ref, oa_ref, od_ref, g_pre_ref, wg_ref, woa_ref, wod_ref, wout_ref, g_post_ref, o_ref):
    r = r_ref[...]
    d = r.shape[-1]
    h = _rms(r, g_pre_ref[...]).astype(BF16)
    g = _dot(h, wg_ref[...])
    a = _dot(oa_ref[...], woa_ref[...])
    bb = _dot(od_ref[...], wod_ref[...])
    mixed = (jax.nn.sigmoid(g[:, :d]) * a + jax.nn.sigmoid(g[:, d:]) * bb).astype(BF16)
    mix = _dot(mixed, wout_ref[...])
    o_ref[...] = r + _rms(mix, g_post_ref[...])


def _merge_call(r2, oa2, od2, g_pre, wg, woa, wod, wout, g_post, tm):
    n, d = r2.shape
    row = lambda w: pl.BlockSpec((tm, w), lambda i: (i, 0))
    return pl.pallas_call(
        _merge_kernel,
        grid=(n // tm,),
        in_specs=[row(d), row(oa2.shape[1]), row(od2.shape[1]),
                  _const_spec(g_pre.shape), _const_spec(wg.shape), _const_spec(woa.shape),
                  _const_spec(wod.shape), _const_spec(wout.shape), _const_spec(g_post.shape)],
        out_specs=row(d),
        out_shape=jax.ShapeDtypeStruct((n, d), F32),
        compiler_params=_params(1),
        name="merge",
    )(r2, oa2, od2, g_pre, wg, woa, wod, wout, g_post)


def _ffn_kernel(r_ref, g_pre_ref, wgu_ref, wd_ref, g_post_ref, o_ref):
    tm = r_ref.shape[0]
    dff = wd_ref.shape[0]
    ng = PROJ_ROW_GROUPS if tm % (PROJ_ROW_GROUPS * LANES) == 0 else 1
    gr = tm // ng
    for gi in range(ng):
        rs = slice(gi * gr, (gi + 1) * gr)
        r = r_ref[rs, :]
        h = _rms(r, g_pre_ref[...]).astype(BF16)
        gu = _dot(h, wgu_ref[...])
        gate = gu[:, :dff]
        act = (gate * jax.nn.sigmoid(gate) * gu[:, dff:]).astype(BF16)
        ff = _dot(act, wd_ref[...])
        o_ref[rs, :] = r + _rms(ff, g_post_ref[...])


def _ffn_call(r2, g_pre, wgu, wd, g_post, tm):
    n, d = r2.shape
    row = pl.BlockSpec((tm, d), lambda i: (i, 0))
    return pl.pallas_call(
        _ffn_kernel,
        grid=(n // tm,),
        in_specs=[row, _const_spec(g_pre.shape), _const_spec(wgu.shape),
                  _const_spec(wd.shape), _const_spec(g_post.shape)],
        out_specs=row,
        out_shape=jax.ShapeDtypeStruct((n, d), F32),
        compiler_params=_params(1),
        name="ffn",
    )(r2, g_pre, wgu, wd, g_post)


def _rope_tables(pos, rot_dim, group_starts):
    half = rot_dim // 2
    lane = jnp.arange(LANES)
    first = jnp.zeros((LANES,), bool)
    second = jnp.zeros((LANES,), bool)
    for g in group_starts:
        first = first | ((lane >= g) & (lane < g + half))
        second = second | ((lane >= g + half) & (lane < g + rot_dim))
    freq = jnp.where(first | second, (lane % half).astype(F32), 0.0)
    inv = ROPE_THETA ** (-(2.0 * freq) / rot_dim)
    ang = pos.astype(F32)[:, None] * inv[None, :]
    cos, sin = jnp.cos(ang), jnp.sin(ang)
    c = jnp.where(first | second, cos, 1.0)
    sa = jnp.where(first, -sin, 0.0)
    sb = jnp.where(second, sin, 0.0)
    return jnp.stack([c, sa, sb])


def _pick_tile(n, candidates):
    for c in candidates:
        if n % c == 0:
            return c
    raise ValueError(f"no tile in {candidates} divides {n}")


def kernel(x, meta_tokens, norm_mix_pre, norm_mix_post, norm_ffn_pre, norm_ffn_post, w_in, mla_q_norm, mla_kv_norm, mla_w_uq, mla_w_ukv, w_o_mla, diff_lambda, diff_subln, w_o_diff, w_out, w_gate_up, w_down):
    dt = x.dtype
    b, s, d = x.shape
    depth = w_in.shape[0]
    r_x = x
    r_m = jnp.zeros((1, META_BLOCK, d), dt).at[0, :N_META].set(meta_tokens.astype(dt))

    t_seq = _pick_tile(s, (512, 256, 128))
    t_tok = _pick_tile(b * s, (512, 256, 128))

    mla_scale = LOG2E / math.sqrt(MLA_NOPE + MLA_ROPE)
    diff_scale = LOG2E / math.sqrt(DIFF_DH)

    def tables(pos):
        return (_rope_tables(pos, MLA_ROPE, (MLA_NOPE,)), _rope_tables(pos, DIFF_ROPE, (0, DIFF_DH)))

    tabs_x = tables(N_META + jnp.arange(s))
    tabs_m = tables(jnp.arange(META_BLOCK))

    c0 = MLA_Q_RANK
    c1 = c0 + MLA_KV_RANK
    c2 = c1 + MLA_ROPE
    c_dv = c2 + 2 * DIFF_W
    c3 = c_dv + DIFF_W
    row1 = lambda v: v.reshape(1, -1).astype(F32)

    for layer in range(depth):
        wl = w_in[layer]
        kpe_w = jnp.zeros((d, LANES), dt).at[:, MLA_NOPE:MLA_NOPE + MLA_ROPE].set(wl[:, c1:c2])
        w1 = jnp.concatenate([wl[:, :c1], kpe_w, wl[:, c2:c_dv]], axis=1).astype(BF16)
        wdvt = wl[:, c_dv:c3].T.astype(BF16)
        wg = wl[:, c3:].astype(BF16)
        wuq = jnp.pad(mla_w_uq[layer].reshape(MLA_Q_RANK, MLA_HEADS, MLA_NOPE + MLA_ROPE),
                      ((0, 0), (0, 0), (0, LANES - MLA_NOPE - MLA_ROPE))
                      ).reshape(MLA_Q_RANK, MLA_W).astype(BF16)
        wukv = mla_w_ukv[layer].reshape(MLA_KV_RANK, MLA_HEADS, MLA_NOPE + MLA_V)
        wk = jnp.pad(wukv[:, :, :MLA_NOPE], ((0, 0), (0, 0), (0, LANES - MLA_NOPE))
                     ).reshape(MLA_KV_RANK, MLA_W).astype(BF16)
        wvt = wukv[:, :, MLA_NOPE:].reshape(MLA_KV_RANK, MLA_HEADS * MLA_V).T.astype(BF16)
        proj_w = (row1(norm_mix_pre[layer]), w1, wdvt, row1(mla_q_norm[layer]), wuq,
                  row1(mla_kv_norm[layer]), wk, wvt)
        merge_w = (row1(norm_mix_pre[layer]), wg, w_o_mla[layer].astype(BF16),
                   w_o_diff[layer].astype(BF16), w_out[layer].astype(BF16), row1(norm_mix_post[layer]))
        ffn_w = (row1(norm_ffn_pre[layer]), w_gate_up[layer].astype(BF16),
                 w_down[layer].astype(BF16), row1(norm_ffn_post[layer]))
        lam_p, g_sub = diff_lambda[layer].astype(F32), row1(diff_subln[layer])
        lam_init = 0.8 - 0.6 * math.exp(-0.3 * layer)

        q, k, vt, dq, dk, dvt = _proj_call(r_x, *proj_w, *tabs_x, t_seq, mla_scale, diff_scale)
        qm, km, vtm, dqm, dkm, dvtm = _proj_call(r_m, *proj_w, *tabs_m, META_BLOCK, mla_scale, diff_scale)

        out_a = _mla_call(q, k, vt, km, vtm, t_seq)
        out_d = _diff_call(dq, dk, dvt, dkm, dvtm, lam_p, g_sub, t_seq, lam_init)
        r2 = _merge_call(r_x.reshape(b * s, d), out_a.reshape(b * s, -1), out_d.reshape(b * s, -1),
                         *merge_w, t_tok)
        r_x = _ffn_call(r2, *ffn_w, t_tok).reshape(b, s, d)

        if layer < depth - 1:
            out_a_m = _mla_meta_call(qm, km, vtm)
            out_d_m = _diff_meta_call(dqm, dkm, dvtm, lam_p, g_sub, lam_init)
            r2 = _merge_call(r_m[0], out_a_m[0], out_d_m[0], *merge_w, META_BLOCK)
            r_m = _ffn_call(r2, *ffn_w, META_BLOCK)[None]

    return r_x
```
